```python
import jax, jax.numpy as jnp
from jax import lax
import numpy as np

D_MODEL = 2048
BATCH = 16
SEQ = 256
DEPTH = 2
DEC_BATCH = 8
DEC_SEQ = 1024
PAST_LEN = 512

GRID_W = 64
N_EVEN = (DEPTH + 1) // 2
N_ODD = DEPTH // 2
NA_HEADS = 8
NA_HEAD_DIM = 128
NA_WIDTH = NA_HEADS * NA_HEAD_DIM
NA_KH_MAX = 8
NA_KW = 16
ATTN_BLOCK = 128
LRU_WIDTH = 1024
LRU_BLOCKS = 8
LRU_BLOCK = LRU_WIDTH // LRU_BLOCKS
LRU_CONV = 4
LRU_C = 8.0
EVEN_IN = 3 * NA_WIDTH + 2 * LRU_WIDTH
EVEN_MIX = NA_WIDTH + LRU_WIDTH
SGU_CHUNK = 128
SGU_WIDTH = 2048
SGU_GROUPS = 16
SGU_GROUP = SGU_WIDTH // SGU_GROUPS
N_EXPERTS = 32
TOP_K = 4
D_EXPERT = 2048
SWIGLU_ALPHA = 1.702
SWIGLU_LIMIT = 7.0
EPS = 1e-6
NEG_INF = -1e30

kernel_name = 'hybrid_na_rglru_sgu_moe_diffusion_step'


def rmsnorm(x, g):
    xf = x.astype(jnp.float32)
    y = xf * lax.rsqrt(jnp.mean(xf * xf, axis=-1, keepdims=True) + EPS)
    return (y * g.astype(jnp.float32)).astype(x.dtype)


def adaln(cond, w, b):
    m = jax.nn.silu(cond) @ w + b
    return jnp.split(m[:, None, :], 6, axis=-1)


def even_project(h, w_in, q_norm, k_norm):
    B, T, _ = h.shape
    proj = h @ w_in
    q, k, v, xr, yg = jnp.split(proj, [NA_WIDTH, 2 * NA_WIDTH, 3 * NA_WIDTH, 3 * NA_WIDTH + LRU_WIDTH], axis=-1)
    q = rmsnorm(q.reshape(B, T, NA_HEADS, NA_HEAD_DIM), q_norm)
    k = rmsnorm(k.reshape(B, T, NA_HEADS, NA_HEAD_DIM), k_norm)
    v = v.reshape(B, T, NA_HEADS, NA_HEAD_DIM)
    return q, k, v, xr, yg


def ctx_attention(q, k, v):
    B, S, H, Dh = q.shape
    scale = Dh ** -0.5
    qb = q.reshape(B, S // ATTN_BLOCK, ATTN_BLOCK, H, Dh).transpose(1, 0, 2, 3, 4)

    def one_block(qblk):
        s = jnp.einsum('bqhd,bkhd->bhqk', qblk, k, preferred_element_type=jnp.float32) * scale
        p = jax.nn.softmax(s, axis=-1)
        return jnp.einsum('bhqk,bkhd->bqhd', p.astype(v.dtype), v)

    o = lax.map(one_block, qb)
    return o.transpose(1, 0, 2, 3, 4).reshape(B, S, H * Dh)


def na_latent_attention(q, k, v, k_ctx, v_ctx, rpb):
    B, T, H, Dh = q.shape
    rows = T // GRID_W
    kh = min(NA_KH_MAX, rows)
    scale = Dh ** -0.5
    qg = q.reshape(B, rows, GRID_W, H, Dh)
    kg = k.reshape(B, rows, GRID_W, H, Dh)
    vg = v.reshape(B, rows, GRID_W, H, Dh)
    qc = jnp.arange(GRID_W)
    c_start = jnp.clip(qc - NA_KW // 2, 0, GRID_W - NA_KW)
    kc = jnp.arange(GRID_W)
    col_ok = (kc[None, :] >= c_start[:, None]) & (kc[None, :] < c_start[:, None] + NA_KW)
    col_idx = jnp.clip(kc[None, :] - qc[:, None] + NA_KW - 1, 0, 2 * NA_KW - 2)

    def one_row(r):
        r_start = jnp.clip(r - kh // 2, 0, rows - kh)
        kw = lax.dynamic_slice_in_dim(kg, r_start, kh, axis=1)
        vw = lax.dynamic_slice_in_dim(vg, r_start, kh, axis=1)
        qr = lax.dynamic_index_in_dim(qg, r, axis=1, keepdims=False)
        s_win = jnp.einsum('bqhd,bakhd->bhqak', qr, kw, preferred_element_type=jnp.float32) * scale
        row_idx = r_start + jnp.arange(kh) - r + NA_KH_MAX - 1
        bias = rpb[:, row_idx[None, :, None], col_idx[:, None, :]].astype(jnp.float32)
        s_win = jnp.where(col_ok[None, None, :, None, :], s_win + bias[None], NEG_INF)
        s_ctx = jnp.einsum('bqhd,bkhd->bhqk', qr, k_ctx, preferred_element_type=jnp.float32) * scale
        s = jnp.concatenate([s_win.reshape(B, H, GRID_W, kh * GRID_W), s_ctx], axis=-1)
        p = jax.nn.softmax(s, axis=-1).astype(v.dtype)
        p_win, p_ctx = p[..., :kh * GRID_W], p[..., kh * GRID_W:]
        o = jnp.einsum('bhqn,bnhd->bqhd', p_win, vw.reshape(B, kh * GRID_W, H, Dh))
        return o + jnp.einsum('bhqk,bkhd->bqhd', p_ctx, v_ctx)

    o = lax.map(one_row, jnp.arange(rows))
    return o.transpose(1, 0, 2, 3, 4).reshape(B, T, H * Dh)


def centred_dwconv(x, conv_w, conv_b):
    T = x.shape[1]
    left = LRU_CONV // 2
    xp = jnp.pad(x, ((0, 0), (left, LRU_CONV - 1 - left), (0, 0)))
    out = conv_b
    for i in range(LRU_CONV):
        out = out + xp[:, i:i + T] * conv_w[i]
    return out


def rglru_coeffs(x, wa, ba, wx, bx, lam):
    B, T, W = x.shape
    xb = x.reshape(B, T, LRU_BLOCKS, LRU_BLOCK)
    r = jax.nn.sigmoid(jnp.einsum('btni,nij->btnj', xb, wa).reshape(B, T, W) + ba)
    i = jax.nn.sigmoid(jnp.einsum('btni,nij->btnj', xb, wx).reshape(B, T, W) + bx)
    log_a = -LRU_C * r.astype(jnp.float32) * jax.nn.softplus(-lam.astype(jnp.float32))
    a = jnp.exp(log_a)
    b = jnp.sqrt(-jnp.expm1(2.0 * log_a)) * (i * x).astype(jnp.float32)
    return a, b


def linear_scan(a, b, h0, reverse):
    def combine(e1, e2):
        a1, b1 = e1
        a2, b2 = e2
        return a1 * a2, a2 * b1 + b2
    a_cum, b_cum = lax.associative_scan(combine, (a, b), axis=1, reverse=reverse)
    return b_cum + a_cum * h0[:, None, :].astype(jnp.float32)


def recurrent_branch(xr, yg, conv_w, conv_b, wa, ba, wx, bx, lam, h0):
    xc = centred_dwconv(xr, conv_w, conv_b)
    af, bf = rglru_coeffs(xc, wa[0], ba[0], wx[0], bx[0], lam[0])
    ab, bb = rglru_coeffs(xc, wa[1], ba[1], wx[1], bx[1], lam[1])
    hf = linear_scan(af, bf, h0[:, 0], reverse=False)
    hb = linear_scan(ab, bb, h0[:, 1], reverse=True)
    out = jax.nn.gelu(yg) * (hf + hb).astype(yg.dtype)
    return out, hf, hb


def sgu_mixer(h, w_in, v_norm, w_s, b_s, w_out):
    B, T, _ = h.shape
    z = jax.nn.gelu(h @ w_in)
    u, vv = jnp.split(z, 2, axis=-1)
    vv = rmsnorm(vv, v_norm)
    vc = vv.reshape(B, T // SGU_CHUNK, SGU_CHUNK, SGU_GROUPS, SGU_GROUP)
    mixed = jnp.einsum('gpq,bcqgd->bcpgd', w_s, vc) + b_s.T[None, None, :, :, None]
    return (u * mixed.reshape(B, T, SGU_WIDTH)) @ w_out


def moe_ffn(h, router_w, router_b, w_gu, b_gu, w_d, b_d):
    B, T, D = h.shape
    xt = h.reshape(B * T, D)
    logits = (xt @ router_w + router_b).astype(jnp.float32)
    top_v, top_i = lax.top_k(logits, TOP_K)
    wts = jax.nn.softmax(top_v, axis=-1)
    gates = jnp.einsum('nk,nke->ne', wts, jax.nn.one_hot(top_i, N_EXPERTS, dtype=jnp.float32))

    def expert(acc, e):
        wgu, bgu, wd, bd, g = e
        glu, lin = jnp.split(xt @ wgu + bgu, 2, axis=-1)
        glu = jnp.minimum(glu, SWIGLU_LIMIT)
        lin = jnp.clip(lin, -SWIGLU_LIMIT, SWIGLU_LIMIT)
        y = (glu * jax.nn.sigmoid(SWIGLU_ALPHA * glu) * (lin + 1.0)) @ wd + bd
        return acc + g[:, None].astype(y.dtype) * y, None

    acc, _ = lax.scan(expert, jnp.zeros_like(xt), (w_gu, b_gu, w_d, b_d, gates.T))
    return acc.reshape(B, T, D)


def setup_inputs(seed: int = 0) -> dict:
    key = jax.random.key(seed)
    ks = iter(jax.random.split(key, 48))
    f32 = jnp.float32

    def nrm(shape, scale):
        return jax.random.normal(next(ks), shape, f32) * scale

    def gain(shape):
        return 1.0 + 0.05 * jax.random.normal(next(ks), shape, f32)

    a0 = jax.random.uniform(next(ks), (N_EVEN, 2, LRU_WIDTH), f32, 0.9, 0.999)
    s0 = a0 ** (1.0 / LRU_C)
    lru_lambda = jnp.log(s0) - jnp.log1p(-s0)
    return {
        'x_prompt': nrm((BATCH, SEQ, D_MODEL), 1.0),
        'x_sample': nrm((DEC_BATCH, DEC_SEQ, D_MODEL), 1.0),
        'cache_na_k': nrm((DEC_BATCH, N_EVEN, PAST_LEN, NA_HEADS, NA_HEAD_DIM), 1.0),
        'cache_na_v': nrm((DEC_BATCH, N_EVEN, PAST_LEN, NA_HEADS, NA_HEAD_DIM), 1.0),
        'state_lru': nrm((DEC_BATCH, N_EVEN, 2, LRU_WIDTH), 0.5),
        'c': nrm((DEC_BATCH, D_MODEL), 1.0),
        'c_ctx': nrm((D_MODEL,), 1.0),
        'norm_mix': gain((DEPTH, D_MODEL)),
        'norm_ffn': gain((DEPTH, D_MODEL)),
        'w_ada': nrm((DEPTH, D_MODEL, 6 * D_MODEL), 0.5 * D_MODEL ** -0.5),
        'b_ada': nrm((DEPTH, 6 * D_MODEL), 0.02),
        'ev_w_in': nrm((N_EVEN, D_MODEL, EVEN_IN), D_MODEL ** -0.5),
        'ev_w_out': nrm((N_EVEN, EVEN_MIX, D_MODEL), EVEN_MIX ** -0.5),
        'na_q_norm': gain((N_EVEN, NA_HEAD_DIM)),
        'na_k_norm': gain((N_EVEN, NA_HEAD_DIM)),
        'na_rpb': nrm((N_EVEN, NA_HEADS, 2 * NA_KH_MAX - 1, 2 * NA_KW - 1), 0.1),
        'lru_conv_w': nrm((N_EVEN, LRU_CONV, LRU_WIDTH), LRU_CONV ** -0.5),
        'lru_conv_b': nrm((N_EVEN, LRU_WIDTH), 0.02),
        'lru_wa': nrm((N_EVEN, 2, LRU_BLOCKS, LRU_BLOCK, LRU_BLOCK), LRU_BLOCK ** -0.5),
        'lru_ba': nrm((N_EVEN, 2, LRU_WIDTH), 0.1),
        'lru_wx': nrm((N_EVEN, 2, LRU_BLOCKS, LRU_BLOCK, LRU_BLOCK), LRU_BLOCK ** -0.5),
        'lru_bx': nrm((N_EVEN, 2, LRU_WIDTH), 0.1),
        'lru_lambda': lru_lambda,
        'od_w_in': nrm((N_ODD, D_MODEL, 2 * SGU_WIDTH), D_MODEL ** -0.5),
        'od_w_out': nrm((N_ODD, SGU_WIDTH, D_MODEL), SGU_WIDTH ** -0.5),
        'sgu_norm': gain((N_ODD, SGU_WIDTH)),
        'sgu_w': nrm((N_ODD, SGU_GROUPS, SGU_CHUNK, SGU_CHUNK), SGU_CHUNK ** -0.5),
        'sgu_b': 1.0 + nrm((N_ODD, SGU_GROUPS, SGU_CHUNK), 0.1),
        'router_w': nrm((DEPTH, D_MODEL, N_EXPERTS), D_MODEL ** -0.5),
        'router_b': nrm((DEPTH, N_EXPERTS), 0.01),
        'moe_w_gu': nrm((DEPTH, N_EXPERTS, D_MODEL, 2 * D_EXPERT), D_MODEL ** -0.5),
        'moe_b_gu': nrm((DEPTH, N_EXPERTS, 2 * D_EXPERT), 0.02),
        'moe_w_down': nrm((DEPTH, N_EXPERTS, D_EXPERT, D_MODEL), D_EXPERT ** -0.5),
        'moe_b_down': nrm((DEPTH, N_EXPERTS, D_MODEL), 0.02),
    }


def reference(x_prompt, x_sample, cache_na_k, cache_na_v, state_lru, c, c_ctx,
              norm_mix, norm_ffn, w_ada, b_ada, ev_w_in, ev_w_out, na_q_norm, na_k_norm, na_rpb,
              lru_conv_w, lru_conv_b, lru_wa, lru_ba, lru_wx, lru_bx, lru_lambda,
              od_w_in, od_w_out, sgu_norm, sgu_w, sgu_b,
              router_w, router_b, moe_w_gu, moe_b_gu, moe_w_down, moe_b_down):
    x = x_prompt
    B, S, _ = x.shape
    new_k, new_v, new_h = [], [], []
    for layer in range(DEPTH):
        sm, scm, gm, sf, scf, gf = adaln(c_ctx[None, :], w_ada[layer], b_ada[layer])
        h = rmsnorm(x, norm_mix[layer]) * (1.0 + scm) + sm
        if layer % 2 == 0:
            e = layer // 2
            q, k, v, xr, yg = even_project(h, ev_w_in[e], na_q_norm[e], na_k_norm[e])
            att = ctx_attention(q, k, v)
            h0 = jnp.zeros((B, 2, LRU_WIDTH), xr.dtype)
            rec, hf, hb = recurrent_branch(xr, yg, lru_conv_w[e], lru_conv_b[e], lru_wa[e], lru_ba[e],
                                           lru_wx[e], lru_bx[e], lru_lambda[e], h0)
            mix = jnp.concatenate([att, rec], axis=-1) @ ev_w_out[e]
            new_k.append(k)
            new_v.append(v)
            new_h.append(jnp.stack([hf[:, -1], hb[:, 0]], axis=1).astype(x.dtype))
        else:
            o = layer // 2
            mix = sgu_mixer(h, od_w_in[o], sgu_norm[o], sgu_w[o], sgu_b[o], od_w_out[o])
        x = x + gm * mix
        h = rmsnorm(x, norm_ffn[layer]) * (1.0 + scf) + sf
        x = x + gf * moe_ffn(h, router_w[layer], router_b[layer], moe_w_gu[layer], moe_b_gu[layer],
                             moe_w_down[layer], moe_b_down[layer])
    y_prompt = x
    new_na_k = jnp.stack(new_k, axis=1)
    new_na_v = jnp.stack(new_v, axis=1)
    new_state_lru = jnp.stack(new_h, axis=1)

    x = x_sample
    for layer in range(DEPTH):
        sm, scm, gm, sf, scf, gf = adaln(c, w_ada[layer], b_ada[layer])
        h = rmsnorm(x, norm_mix[layer]) * (1.0 + scm) + sm
        if layer % 2 == 0:
            e = layer // 2
            q, k, v, xr, yg = even_project(h, ev_w_in[e], na_q_norm[e], na_k_norm[e])
            att = na_latent_attention(q, k, v, cache_na_k[:, e], cache_na_v[:, e], na_rpb[e])
            rec, _, _ = recurrent_branch(xr, yg, lru_conv_w[e], lru_conv_b[e], lru_wa[e], lru_ba[e],
                                         lru_wx[e], lru_bx[e], lru_lambda[e], state_lru[:, e])
            mix = jnp.concatenate([att, rec], axis=-1) @ ev_w_out[e]
        else:
            o = layer // 2
            mix = sgu_mixer(h, od_w_in[o], sgu_norm[o], sgu_w[o], sgu_b[o], od_w_out[o])
        x = x + gm * mix
        h = rmsnorm(x, norm_ffn[layer]) * (1.0 + scf) + sf
        x = x + gf * moe_ffn(h, router_w[layer], router_b[layer], moe_w_gu[layer], moe_b_gu[layer],
                             moe_w_down[layer], moe_b_down[layer])
    y_sample = x
    return (y_prompt, y_sample, new_na_k, new_na_v, new_state_lru)
```

```python
import functools

import numpy as np
import jax
import jax.numpy as jnp
from jax import lax
from jax.experimental import pallas as pl
from jax.experimental.pallas import tpu as pltpu

F32 = jnp.float32
BF16 = jnp.bfloat16

EPS = 1e-6
NEG_INF = -1e30
GRID_W = 64
NA_KH = 8
NA_KW = 16
LRU_C = 8.0
SWIGLU_ALPHA = 1.702
SWIGLU_LIMIT = 7.0
TOP_K = 4

LANES = 128
ROW_TILE = 256
MOE_SUB = 256
MOE_ITEM_SUBS = 8
MOE_TF = 256
VMEM_LIMIT = 56 * 1024 * 1024


def _cparams(sem, vmem=VMEM_LIMIT):
    return pltpu.CompilerParams(dimension_semantics=sem, vmem_limit_bytes=vmem)


def _split_bf16(x):
    hi = x.astype(BF16)
    lo = (x - hi.astype(F32)).astype(BF16)
    return hi, lo


def _gelu_tanh(x):
    return 0.5 * x * (1.0 + jnp.tanh(0.7978845608028654 * (x + 0.044715 * x * x * x)))


def _dot(a, b):
    return jnp.dot(a, b, preferred_element_type=F32)


def _dot_nt(a, b):
    return lax.dot_general(a, b, (((1,), (1,)), ((), ())), preferred_element_type=F32)


def _adaln_kernel(c_ref, w_ref, b_ref, o_ref):
    cnd = c_ref[...]
    s = cnd * jax.nn.sigmoid(cnd)
    hi, lo = _split_bf16(s)
    w = w_ref[...].astype(BF16)
    o_ref[...] = _dot(hi, w) + _dot(lo, w) + b_ref[...]


def _adaln(cond, w_ada, b_ada, tn=1024):
    depth, d, n6 = w_ada.shape
    rows = cond.shape[0]
    return pl.pallas_call(
        _adaln_kernel,
        grid=(depth, n6 // tn),
        in_specs=[
            pl.BlockSpec((rows, d), lambda l, j: (0, 0)),
            pl.BlockSpec((None, d, tn), lambda l, j: (l, 0, j)),
            pl.BlockSpec((None, 1, tn), lambda l, j: (l, 0, j)),
        ],
        out_specs=pl.BlockSpec((None, rows, tn), lambda l, j: (l, 0, j)),
        out_shape=jax.ShapeDtypeStruct((depth, rows, n6), F32),
        compiler_params=_cparams(("arbitrary", "arbitrary")),
        name="adaln",
    )(cond, w_ada, b_ada.reshape(depth, 1, n6))


def _modulated_norm(x, g, sc, sh):
    ms = jnp.mean(x * x, axis=-1, keepdims=True)
    return (x * lax.rsqrt(ms + EPS) * g) * (1.0 + sc) + sh


def _inproj_kernel(tc_ref, x_ref, g_ref, sc_ref, sh_ref, w_ref, hg_ref, o_ref, h_scr, *,
                   mode, n_norm_tiles):
    j = pl.program_id(1)

    @pl.when(j == 0)
    def _():
        h_scr[...] = _modulated_norm(x_ref[...], g_ref[...], sc_ref[...], sh_ref[...]).astype(BF16)

    acc = _dot(h_scr[...], w_ref[...])
    if mode == "gelu":
        o_ref[...] = _gelu_tanh(acc)
    else:
        @pl.when(j < n_norm_tiles)
        def _():
            hg = hg_ref[...]
            for c in range(acc.shape[1] // LANES):
                blk = acc[:, c * LANES:(c + 1) * LANES]
                ms = jnp.mean(blk * blk, axis=-1, keepdims=True)
                o_ref[:, c * LANES:(c + 1) * LANES] = (
                    blk * lax.rsqrt(ms + EPS) * hg[:, c * LANES:(c + 1) * LANES])

        @pl.when(j >= n_norm_tiles)
        def _():
            o_ref[...] = acc


def _inproj(x, tile_cond, norm_g, sc, sh, w_bf16, head_gain, *, mode, n_norm_cols, tn=512):
    n, d = x.shape
    nout = w_bf16.shape[1]
    tm = ROW_TILE
    kern = functools.partial(_inproj_kernel, mode=mode, n_norm_tiles=n_norm_cols // tn)
    modspec = pl.BlockSpec((None, 1, d), lambda i, j, tc: (tc[i], 0, 0))
    return pl.pallas_call(
        kern,
        grid_spec=pltpu.PrefetchScalarGridSpec(
            num_scalar_prefetch=1,
            grid=(n // tm, nout // tn),
            in_specs=[
                pl.BlockSpec((tm, d), lambda i, j, tc: (i, 0)),
                pl.BlockSpec((1, d), lambda i, j, tc: (0, 0)),
                modspec, modspec,
                pl.BlockSpec((d, tn), lambda i, j, tc: (0, j)),
                pl.BlockSpec((1, tn), lambda i, j, tc: (0, j)),
            ],
            out_specs=pl.BlockSpec((tm, tn), lambda i, j, tc: (i, j)),
            scratch_shapes=[pltpu.VMEM((tm, d), BF16)],
        ),
        out_shape=jax.ShapeDtypeStruct((n, nout), F32),
        compiler_params=_cparams(("arbitrary", "arbitrary")),
        name="inproj_" + mode,
    )(tile_cond, x, norm_g, sc, sh, w_bf16, head_gain)


def _ctx_attn_kernel(q_ref, k_ref, v_ref, o_ref, *, heads, dh):
    scale = dh ** -0.5
    for h in range(heads):
        sl = slice(h * dh, (h + 1) * dh)
        q = q_ref[:, sl].astype(BF16)
        k = k_ref[:, sl].astype(BF16)
        v = v_ref[:, sl].astype(BF16)
        s = _dot_nt(q, k) * scale
        m = jnp.max(s, axis=-1, keepdims=True)
        p = jnp.exp(s - m)
        den = jnp.sum(p, axis=-1, keepdims=True)
        o = _dot(p.astype(BF16), v) / den
        o_ref[:, sl] = o.astype(o_ref.dtype)


def _ctx_attention(proj, *, row0, nseq, seq, heads, dh):
    width = heads * dh
    rb0 = row0 // seq
    kern = functools.partial(_ctx_attn_kernel, heads=heads, dh=dh)
    return pl.pallas_call(
        kern,
        grid=(nseq,),
        in_specs=[
            pl.BlockSpec((seq, width), lambda b: (rb0 + b, 0)),
            pl.BlockSpec((seq, width), lambda b: (rb0 + b, 1)),
            pl.BlockSpec((seq, width), lambda b: (rb0 + b, 2)),
        ],
        out_specs=pl.BlockSpec((seq, width), lambda b: (b, 0)),
        out_shape=jax.ShapeDtypeStruct((nseq * seq, width), BF16),
        compiler_params=_cparams(("arbitrary",)),
        name="ctx_attention",
    )(proj, proj, proj)


def _na_attn_kernel(q_ref, k_ref, v_ref, kc_ref, vc_ref, bias_ref, o_ref, *, rows, dh):
    scale = dh ** -0.5
    win = NA_KH * GRID_W
    qc = lax.broadcasted_iota(jnp.int32, (GRID_W, win), 0)
    kc = lax.broadcasted_iota(jnp.int32, (GRID_W, win), 1) % GRID_W
    c_start = jnp.clip(qc - NA_KW // 2, 0, GRID_W - NA_KW)
    col_ok = (kc >= c_start) & (kc < c_start + NA_KW)
    k_ctx = kc_ref[...].astype(BF16)
    v_ctx = vc_ref[...].astype(BF16)
    for r in range(rows):
        rs = min(max(r - NA_KH // 2, 0), rows - NA_KH)
        q = q_ref[r * GRID_W:(r + 1) * GRID_W, :].astype(BF16)
        kw = k_ref[rs * GRID_W:rs * GRID_W + win, :].astype(BF16)
        vw = v_ref[rs * GRID_W:rs * GRID_W + win, :].astype(BF16)
        s_win = _dot_nt(q, kw) * scale + bias_ref[r - rs]
        s_win = jnp.where(col_ok, s_win, NEG_INF)
        s_ctx = _dot_nt(q, k_ctx) * scale
        m = jnp.maximum(jnp.max(s_win, axis=-1, keepdims=True),
                        jnp.max(s_ctx, axis=-1, keepdims=True))
        p_win = jnp.exp(s_win - m)
        p_ctx = jnp.exp(s_ctx - m)
        den = jnp.sum(p_win, axis=-1, keepdims=True) + jnp.sum(p_ctx, axis=-1, keepdims=True)
        o = (_dot(p_win.astype(BF16), vw) + _dot(p_ctx.astype(BF16), v_ctx)) / den
        o_ref[r * GRID_W:(r + 1) * GRID_W, :] = o.astype(o_ref.dtype)


def _na_bias_table(rpb):
    qc = np.arange(GRID_W)
    col_idx = np.clip(qc[None, :] - qc[:, None] + NA_KW - 1, 0, 2 * NA_KW - 2)
    a = np.arange(NA_KH)
    row_idx = a[None, :] - a[:, None] + NA_KH - 1
    tab = rpb[:, row_idx[:, None, :, None], col_idx[None, :, None, :]]
    return tab.reshape(rpb.shape[0], NA_KH, GRID_W, NA_KH * GRID_W).astype(F32)


def _na_attention(proj, cache_k, cache_v, bias_tab, *, nseq, seq, heads, dh):
    past = cache_k.shape[1]
    rows = seq // GRID_W
    kern = functools.partial(_na_attn_kernel, rows=rows, dh=dh)
    return pl.pallas_call(
        kern,
        grid=(heads, nseq),
        in_specs=[
            pl.BlockSpec((seq, dh), lambda h, b: (b, h)),
            pl.BlockSpec((seq, dh), lambda h, b: (b, heads + h)),
            pl.BlockSpec((seq, dh), lambda h, b: (b, 2 * heads + h)),
            pl.BlockSpec((None, past, dh), lambda h, b: (b, 0, h)),
            pl.BlockSpec((None, past, dh), lambda h, b: (b, 0, h)),
            pl.BlockSpec((None, NA_KH, GRID_W, NA_KH * GRID_W), lambda h, b: (h, 0, 0, 0)),
        ],
        out_specs=pl.BlockSpec((seq, dh), lambda h, b: (b, h)),
        out_shape=jax.ShapeDtypeStruct((nseq * seq, heads * dh), BF16),
        compiler_params=_cparams(("arbitrary", "arbitrary")),
        name="na_attention",
    )(proj, proj, proj, cache_k, cache_v, bias_tab)


REC_SEQS = 8
REC_CHUNK = 256


def _rec_kernel(xr_ref, yg_ref, cw_ref, cb_ref, wg_ref, bg_ref, lam_ref, h0_ref,
                out_ref, hl_ref, a_scr, b_scr, *, seq):
    pitch = seq + 8
    total = REC_SEQS * seq
    nchunks = total // REC_CHUNK
    row = lax.broadcasted_iota(jnp.int32, (REC_CHUNK, LANES), 0)
    cw = cw_ref[...]
    cb = cb_ref[...]
    wg_hi = wg_ref[0]
    wg_lo = wg_ref[1]
    bg = bg_ref[...]
    lam = lam_ref[...]
    sp = jnp.maximum(-lam, 0.0) + jnp.log(1.0 + jnp.exp(-jnp.abs(lam)))

    def coeff_chunk(ci, carry):
        r0 = pl.multiple_of(ci * REC_CHUNK, REC_CHUNK)
        t0 = r0 % seq
        x = xr_ref[pl.ds(r0, REC_CHUNK), :]
        prev = xr_ref[pl.ds(pl.multiple_of(jnp.maximum(r0 - 8, 0), 8), 8), :]
        nxt = xr_ref[pl.ds(pl.multiple_of(jnp.minimum(r0 + REC_CHUNK, total - 8), 8), 8), :]
        has_prev = (t0 > 0).astype(F32)
        has_next = (t0 + REC_CHUNK < seq).astype(F32)
        p6 = prev[6:7, :] * has_prev
        p7 = prev[7:8, :] * has_prev
        n0 = nxt[0:1, :] * has_next
        x_m1 = jnp.where(row == 0, p7, pltpu.roll(x, 1, axis=0))
        x_m2 = jnp.where(row == 0, p6, jnp.where(row == 1, p7, pltpu.roll(x, 2, axis=0)))
        x_p1 = jnp.where(row == REC_CHUNK - 1, n0, pltpu.roll(x, REC_CHUNK - 1, axis=0))
        xc = cb + x_m2 * cw[0:1, :] + x_m1 * cw[1:2, :] + x * cw[2:3, :] + x_p1 * cw[3:4, :]
        xh, xl = _split_bf16(xc)
        gates = _dot(xh, wg_hi) + _dot(xl, wg_hi) + _dot(xh, wg_lo) + bg
        srow = pl.multiple_of((r0 // seq) * pitch + t0, 8)
        for d in range(2):
            r = jax.nn.sigmoid(gates[:, (2 * d) * LANES:(2 * d + 1) * LANES])
            i = jax.nn.sigmoid(gates[:, (2 * d + 1) * LANES:(2 * d + 2) * LANES])
            log_a = (-LRU_C) * r * sp[d:d + 1, :]
            a = jnp.exp(log_a)
            b = jnp.sqrt(1.0 - jnp.exp(2.0 * log_a)) * (i * xc)
            a_scr[d, pl.ds(srow, REC_CHUNK), :] = a
            b_scr[d, pl.ds(srow, REC_CHUNK), :] = b
        return carry

    lax.fori_loop(0, nchunks, coeff_chunk, 0)

    def scan_step(t, carry):
        hf, hb = carry
        tb = seq - 1 - t
        hf = a_scr[0, pl.ds(t, REC_SEQS, stride=pitch), :] * hf + b_scr[0, pl.ds(t, REC_SEQS, stride=pitch), :]
        b_scr[0, pl.ds(t, REC_SEQS, stride=pitch), :] = hf
        hb = a_scr[1, pl.ds(tb, REC_SEQS, stride=pitch), :] * hb + b_scr[1, pl.ds(tb, REC_SEQS, stride=pitch), :]
        b_scr[1, pl.ds(tb, REC_SEQS, stride=pitch), :] = hb
        return hf, hb

    hf, hb = lax.fori_loop(0, seq, scan_step, (h0_ref[0], h0_ref[1]), unroll=8)
    hl_ref[0] = hf
    hl_ref[1] = hb

    def out_chunk(ci, carry):
        r0 = pl.multiple_of(ci * REC_CHUNK, REC_CHUNK)
        srow = pl.multiple_of((r0 // seq) * pitch + r0 % seq, 8)
        hsum = b_scr[0, pl.ds(srow, REC_CHUNK), :] + b_scr[1, pl.ds(srow, REC_CHUNK), :]
        out_ref[pl.ds(r0, REC_CHUNK), :] = (_gelu_tanh(yg_ref[pl.ds(r0, REC_CHUNK), :]) * hsum).astype(out_ref.dtype)
        return carry

    lax.fori_loop(0, nchunks, out_chunk, 0)


def _recurrent(proj, conv_w, conv_b, wgate, bgate, lam, h0, *, row0, nseq, seq, xr_col, yg_col, width):
    groups = nseq // REC_SEQS
    blk_rows = REC_SEQS * seq
    rb0 = row0 // blk_rows
    nblk = width // LANES
    xc0 = xr_col // LANES
    yc0 = yg_col // LANES
    pitch = seq + 8
    kern = functools.partial(_rec_kernel, seq=seq)
    return pl.pallas_call(
        kern,
        grid=(groups, nblk),
        in_specs=[
            pl.BlockSpec((blk_rows, LANES), lambda g, n: (rb0 + g, xc0 + n)),
            pl.BlockSpec((blk_rows, LANES), lambda g, n: (rb0 + g, yc0 + n)),
            pl.BlockSpec((4, LANES), lambda g, n: (0, n)),
            pl.BlockSpec((1, LANES), lambda g, n: (0, n)),
            pl.BlockSpec((None, 2, LANES, 4 * LANES), lambda g, n: (n, 0, 0, 0)),
            pl.BlockSpec((None, 1, 4 * LANES), lambda g, n: (n, 0, 0)),
            pl.BlockSpec((2, LANES), lambda g, n: (0, n)),
            pl.BlockSpec((None, 2, REC_SEQS, LANES), lambda g, n: (g, 0, 0, n)),
        ],
        out_specs=[
            pl.BlockSpec((blk_rows, LANES), lambda g, n: (g, n)),
            pl.BlockSpec((None, 2, REC_SEQS, LANES), lambda g, n: (g, 0, 0, n)),
        ],
        out_shape=[
            jax.ShapeDtypeStruct((nseq * seq, width), BF16),
            jax.ShapeDtypeStruct((groups, 2, REC_SEQS, width), F32),
        ],
        scratch_shapes=[
            pltpu.VMEM((2, REC_SEQS * pitch, LANES), F32),
            pltpu.VMEM((2, REC_SEQS * pitch, LANES), F32),
        ],
        compiler_params=_cparams(("arbitrary", "arbitrary")),
        name="recurrent",
    )(proj, proj, conv_w, conv_b, wgate, bgate, lam, h0)


def _pack_gate_weights(wa, ba, wx, bx):
    nblk = wa.shape[1]
    w = jnp.concatenate([wa[0], wx[0], wa[1], wx[1]], axis=-1)
    hi, lo = _split_bf16(w)
    b = jnp.stack([ba[0].reshape(nblk, LANES), bx[0].reshape(nblk, LANES),
                   ba[1].reshape(nblk, LANES), bx[1].reshape(nblk, LANES)], axis=1)
    return jnp.stack([hi, lo], axis=1), b.reshape(nblk, 1, 4 * LANES)


def _sgu_kernel(u_ref, v_ref, g_ref, ws_ref, bs_ref, o_ref, *, chunk, groups):
    g = g_ref[...]
    for c in range(u_ref.shape[0] // chunk):
        rs = slice(c * chunk, (c + 1) * chunk)
        v = v_ref[rs, :]
        ms = jnp.mean(v * v, axis=-1, keepdims=True)
        vn = (v * lax.rsqrt(ms + EPS) * g).astype(BF16)
        for grp in range(groups):
            cs = slice(grp * LANES, (grp + 1) * LANES)
            mixed = _dot(ws_ref[grp], vn[:, cs]) + bs_ref[:, grp:grp + 1]
            o_ref[rs, cs] = (u_ref[rs, cs] * mixed).astype(o_ref.dtype)


def _sgu(z, v_norm, ws_bf16, bs_t, *, chunk, tm=512):
    n, w2 = z.shape
    w = w2 // 2
    groups = ws_bf16.shape[0]
    kern = functools.partial(_sgu_kernel, chunk=chunk, groups=groups)
    return pl.pallas_call(
        kern,
        grid=(n // tm,),
        in_specs=[
            pl.BlockSpec((tm, w), lambda i: (i, 0)),
            pl.BlockSpec((tm, w), lambda i: (i, 1)),
            pl.BlockSpec((1, w), lambda i: (0, 0)),
            pl.BlockSpec((groups, chunk, chunk), lambda i: (0, 0, 0)),
            pl.BlockSpec((chunk, groups), lambda i: (0, 0)),
        ],
        out_specs=pl.BlockSpec((tm, w), lambda i: (i, 0)),
        out_shape=jax.ShapeDtypeStruct((n, w), BF16),
        compiler_params=_cparams(("arbitrary",)),
        name="sgu",
    )(z, z, v_norm, ws_bf16, bs_t)


def _outproj_kernel(tc_ref, *refs, n_lhs, n_experts):
    lhs = refs[:n_lhs]
    ws = refs[n_lhs:2 * n_lhs]
    (x_ref, gm_ref, g2_ref, scf_ref, sf_ref, rwh_ref, rwl_ref, rb_ref,
     xo_ref, h2_ref, idx_ref, wts_ref) = refs[2 * n_lhs:]
    mix = _dot(lhs[0][...], ws[0][...])
    for a_ref, w_ref in zip(lhs[1:], ws[1:]):
        mix = mix + _dot(a_ref[...], w_ref[...])
    xn = x_ref[...] + gm_ref[...] * mix
    xo_ref[...] = xn
    h2 = _modulated_norm(xn, g2_ref[...], scf_ref[...], sf_ref[...])
    hi, lo = _split_bf16(h2)
    h2_ref[...] = hi
    logits = _dot(hi, rwh_ref[...]) + _dot(lo, rwh_ref[...]) + _dot(hi, rwl_ref[...]) + rb_ref[...]
    lane = lax.broadcasted_iota(jnp.int32, logits.shape, 1)
    l = jnp.where(lane < n_experts, logits, -jnp.inf)
    vals, idxs = [], []
    for _ in range(TOP_K):
        m = jnp.max(l, axis=-1, keepdims=True)
        i = jnp.min(jnp.where(l == m, lane, LANES), axis=-1, keepdims=True)
        vals.append(m)
        idxs.append(i)
        l = jnp.where(lane == i, -jnp.inf, l)
    es = [jnp.exp(v - vals[0]) for v in vals]
    den = es[0] + es[1] + es[2] + es[3]
    idx_out = jnp.zeros(logits.shape, jnp.int32)
    wts_out = jnp.zeros(logits.shape, F32)
    for k in range(TOP_K):
        idx_out = jnp.where(lane == k, idxs[k], idx_out)
        wts_out = jnp.where(lane == k, es[k] / den, wts_out)
    idx_ref[...] = idx_out
    wts_ref[...] = wts_out


def _outproj(lhs_list, w_list, x, tile_cond, gm, g2, scf, sf, rw_hi, rw_lo, rb, *, n_experts):
    n, d = x.shape
    tm = ROW_TILE
    n_lhs = len(lhs_list)
    kern = functools.partial(_outproj_kernel, n_lhs=n_lhs, n_experts=n_experts)
    modspec = pl.BlockSpec((None, 1, d), lambda i, tc: (tc[i], 0, 0))
    in_specs = (
        [pl.BlockSpec((tm, a.shape[1]), lambda i, tc: (i, 0)) for a in lhs_list]
        + [pl.BlockSpec(w.shape, lambda i, tc: (0, 0)) for w in w_list]
        + [pl.BlockSpec((tm, d), lambda i, tc: (i, 0)),
           modspec,
           pl.BlockSpec((1, d), lambda i, tc: (0, 0)),
           modspec, modspec,
           pl.BlockSpec((d, LANES), lambda i, tc: (0, 0)),
           pl.BlockSpec((d, LANES), lambda i, tc: (0, 0)),
           pl.BlockSpec((1, LANES), lambda i, tc: (0, 0))])
    return pl.pallas_call(
        kern,
        grid_spec=pltpu.PrefetchScalarGridSpec(
            num_scalar_prefetch=1,
            grid=(n // tm,),
            in_specs=in_specs,
            out_specs=[
                pl.BlockSpec((tm, d), lambda i, tc: (i, 0)),
                pl.BlockSpec((tm, d), lambda i, tc: (i, 0)),
                pl.BlockSpec((tm, LANES), lambda i, tc: (i, 0)),
                pl.BlockSpec((tm, LANES), lambda i, tc: (i, 0)),
            ],
        ),
        out_shape=[
            jax.ShapeDtypeStruct((n, d), F32),
            jax.ShapeDtypeStruct((n, d), BF16),
            jax.ShapeDtypeStruct((n, LANES), jnp.int32),
            jax.ShapeDtypeStruct((n, LANES), F32),
        ],
        compiler_params=_cparams(("arbitrary",)),
        name="outproj_router",
    )(tile_cond, *lhs_list, *w_list, x, gm, g2, scf, sf, rw_hi, rw_lo, rb)


def _moe_kernel(ie_ref, it0_ref, int_ref, inz_ref, xs_hbm, wg_ref, wl_ref, wd_ref, bg_ref, bl_ref, bd_ref,
                ys_hbm, xbuf, acc, wgb, wlb, wdb, sem_in, sem_out, *, n_chunks):
    s = pl.program_id(0)
    c = pl.program_id(1)
    nt = int_ref[s]
    nz = inz_ref[s]
    t0 = it0_ref[s]

    @pl.when((c == 0) & (nz > 0))
    def _():
        acc[0] = jnp.zeros(acc.shape[1:], acc.dtype)

        def copy_zero(j):
            return pltpu.make_async_copy(
                acc.at[0], ys_hbm.at[pl.ds((t0 + j) * MOE_SUB, MOE_SUB), :], sem_out.at[j])

        for j in range(MOE_ITEM_SUBS):
            @pl.when(j < nz)
            def _():
                copy_zero(j).start()
        for j in range(MOE_ITEM_SUBS):
            @pl.when(j < nz)
            def _():
                copy_zero(j).wait()

    def copy_in(j):
        return pltpu.make_async_copy(
            xs_hbm.at[pl.ds((t0 + j) * MOE_SUB, MOE_SUB), :], xbuf.at[j], sem_in.at[j])

    def copy_out(j):
        return pltpu.make_async_copy(
            acc.at[j], ys_hbm.at[pl.ds((t0 + j) * MOE_SUB, MOE_SUB), :], sem_out.at[j])

    @pl.when(c == 0)
    def _():
        for j in range(MOE_ITEM_SUBS):
            @pl.when(j < nt)
            def _():
                copy_in(j).start()
        for j in range(MOE_ITEM_SUBS):
            @pl.when(j < nt)
            def _():
                copy_in(j).wait()

    @pl.when(nt > 0)
    def _():
        wgb[...] = wg_ref[...].astype(BF16)
        wlb[...] = wl_ref[...].astype(BF16)
        wdb[...] = wd_ref[...].astype(BF16)
        bg = bg_ref[...]
        bl = bl_ref[...]

        def sub_tile(j, carry):
            x = xbuf[j]
            glu = jnp.minimum(_dot(x, wgb[...]) + bg, SWIGLU_LIMIT)
            lin = jnp.clip(_dot(x, wlb[...]) + bl, -SWIGLU_LIMIT, SWIGLU_LIMIT)
            act = (glu * jax.nn.sigmoid(SWIGLU_ALPHA * glu) * (lin + 1.0)).astype(BF16)
            contrib = _dot(act, wdb[...])

            @pl.when(c == 0)
            def _():
                acc[j] = contrib

            @pl.when(c > 0)
            def _():
                acc[j] += contrib
            return carry

        lax.fori_loop(0, nt, sub_tile, 0)

    @pl.when(c == n_chunks - 1)
    def _():
        bd = bd_ref[...]
        for j in range(MOE_ITEM_SUBS):
            @pl.when(j < nt)
            def _():
                acc[j] += bd
                copy_out(j).start()
        for j in range(MOE_ITEM_SUBS):
            @pl.when(j < nt)
            def _():
                copy_out(j).wait()


def _moe_experts(xs, items, w_gu, b_gu, w_down, b_down, layer):
    rp, d = xs.shape
    _, n_exp, _, f2 = w_gu.shape
    f = f2 // 2
    n_chunks = f // MOE_TF
    item_e, item_t0, item_nt, item_nz = items
    n_items = item_e.shape[0]

    def cidx(c, nt_s):
        return jnp.where(nt_s > 0, c, n_chunks - 1)

    kern = functools.partial(_moe_kernel, n_chunks=n_chunks)
    return pl.pallas_call(
        kern,
        grid_spec=pltpu.PrefetchScalarGridSpec(
            num_scalar_prefetch=4,
            grid=(n_items, n_chunks),
            in_specs=[
                pl.BlockSpec(memory_space=pl.ANY),
                pl.BlockSpec((None, None, d, MOE_TF),
                             lambda s, c, ie, it0, nt, nz:(layer, ie[s], 0, cidx(c, nt[s]))),
                pl.BlockSpec((None, None, d, MOE_TF),
                             lambda s, c, ie, it0, nt, nz:(layer, ie[s], 0, n_chunks + cidx(c, nt[s]))),
                pl.BlockSpec((None, None, MOE_TF, d),
                             lambda s, c, ie, it0, nt, nz:(layer, ie[s], cidx(c, nt[s]), 0)),
                pl.BlockSpec((None, None, 1, MOE_TF),
                             lambda s, c, ie, it0, nt, nz:(layer, ie[s], 0, cidx(c, nt[s]))),
                pl.BlockSpec((None, None, 1, MOE_TF),
                             lambda s, c, ie, it0, nt, nz:(layer, ie[s], 0, n_chunks + cidx(c, nt[s]))),
                pl.BlockSpec((None, None, 1, d),
                             lambda s, c, ie, it0, nt, nz:(layer, ie[s], 0, 0)),
            ],
            out_specs=pl.BlockSpec(memory_space=pl.ANY),
            scratch_shapes=[
                pltpu.VMEM((MOE_ITEM_SUBS, MOE_SUB, d), BF16),
                pltpu.VMEM((MOE_ITEM_SUBS, MOE_SUB, d), F32),
                pltpu.VMEM((d, MOE_TF), BF16),
                pltpu.VMEM((d, MOE_TF), BF16),
                pltpu.VMEM((MOE_TF, d), BF16),
                pltpu.SemaphoreType.DMA((MOE_ITEM_SUBS,)),
                pltpu.SemaphoreType.DMA((MOE_ITEM_SUBS,)),
            ],
        ),
        out_shape=jax.ShapeDtypeStruct((rp, d), F32),
        compiler_params=_cparams(("arbitrary", "arbitrary")),
        name="moe_experts",
    )(item_e, item_t0, item_nt, item_nz, xs, w_gu, w_gu, w_down,
      b_gu.reshape(b_gu.shape[0], n_exp, 1, f2), b_gu.reshape(b_gu.shape[0], n_exp, 1, f2),
      b_down.reshape(b_down.shape[0], n_exp, 1, d))


def _moe_plan(top_i, n_experts, rows_padded):
    n = top_i.shape[0]
    e_flat = top_i.reshape(-1)
    onehot = (e_flat[:, None] == jnp.arange(n_experts, dtype=jnp.int32)[None, :]).astype(jnp.int32)
    csum = jnp.cumsum(onehot, axis=0)
    counts = csum[-1]
    rank = jnp.sum(csum * onehot, axis=1) - 1
    ntiles = (counts + MOE_SUB - 1) // MOE_SUB
    tile_end = jnp.cumsum(ntiles)
    tile_start = tile_end - ntiles
    pos = tile_start[e_flat] * MOE_SUB + rank
    tok = jnp.arange(n * TOP_K, dtype=jnp.int32) // TOP_K
    row_token = jnp.zeros((rows_padded,), jnp.int32).at[pos].set(tok, unique_indices=True)
    n_items_max = n_experts + (n * TOP_K // MOE_SUB + MOE_ITEM_SUBS - 1) // MOE_ITEM_SUBS
    items_e = (ntiles + MOE_ITEM_SUBS - 1) // MOE_ITEM_SUBS
    item_end = jnp.cumsum(items_e)
    item_start = item_end - items_e
    sidx = jnp.arange(n_items_max, dtype=jnp.int32)
    valid = sidx < item_end[-1]
    ie = jnp.sum((sidx[:, None] >= item_end[None, :]).astype(jnp.int32), axis=1)
    ie = jnp.minimum(ie, n_experts - 1)
    local = sidx - item_start[ie]
    t0 = tile_start[ie] + local * MOE_ITEM_SUBS
    nt = jnp.clip(ntiles[ie] - local * MOE_ITEM_SUBS, 0, MOE_ITEM_SUBS)
    last_e = ie[jnp.maximum(item_end[-1] - 1, 0)]
    ie = jnp.where(valid, ie, last_e).astype(jnp.int32)
    z0 = tile_end[-1] + (sidx - item_end[-1]) * MOE_ITEM_SUBS
    nz = jnp.clip(rows_padded // MOE_SUB - z0, 0, MOE_ITEM_SUBS)
    t0 = jnp.where(valid, t0, z0).astype(jnp.int32)
    nt = jnp.where(valid, nt, 0).astype(jnp.int32)
    nz = jnp.where(valid, 0, nz).astype(jnp.int32)
    return pos.reshape(n, TOP_K), row_token, (ie, t0, nt, nz)


def _combine_kernel(tc_ref, x_ref, y_ref, w_ref, gf_ref, o_ref):
    w = w_ref[...]
    acc = w[:, 0:1] * y_ref[:, 0, :]
    for k in range(1, TOP_K):
        acc = acc + w[:, k:k + 1] * y_ref[:, k, :]
    o_ref[...] = x_ref[...] + gf_ref[...] * acc


def _combine(x, yk, wts, tile_cond, gf):
    n, d = x.shape
    tm = ROW_TILE
    return pl.pallas_call(
        _combine_kernel,
        grid_spec=pltpu.PrefetchScalarGridSpec(
            num_scalar_prefetch=1,
            grid=(n // tm,),
            in_specs=[
                pl.BlockSpec((tm, d), lambda i, tc: (i, 0)),
                pl.BlockSpec((tm, TOP_K, d), lambda i, tc: (i, 0, 0)),
                pl.BlockSpec((tm, LANES), lambda i, tc: (i, 0)),
                pl.BlockSpec((None, 1, d), lambda i, tc: (tc[i], 0, 0)),
            ],
            out_specs=pl.BlockSpec((tm, d), lambda i, tc: (i, 0)),
        ),
        out_shape=jax.ShapeDtypeStruct((n, d), F32),
        compiler_params=_cparams(("arbitrary",)),
        name="moe_combine",
    )(tile_cond, x, yk, wts, gf)


def _moe(x, h2, idx, wts, tile_cond, gf, w_gu, b_gu, w_down, b_down, layer):
    n, d = x.shape
    n_exp = w_gu.shape[1]
    rows_padded = n * TOP_K + n_exp * MOE_SUB
    pos, row_token, items = _moe_plan(idx[:, :TOP_K], n_exp, rows_padded)
    xs = jnp.take(h2, row_token, axis=0)
    ys = _moe_experts(xs, items, w_gu, b_gu, w_down, b_down, layer)
    yk = jnp.take(ys, pos.reshape(-1), axis=0).reshape(n, TOP_K, d)
    return _combine(x, yk, wts, tile_cond, gf)


def kernel(x_prompt, x_sample, cache_na_k, cache_na_v, state_lru, c, c_ctx, norm_mix, norm_ffn, w_ada, b_ada, ev_w_in, ev_w_out, na_q_norm, na_k_norm, na_rpb, lru_conv_w, lru_conv_b, lru_wa, lru_ba, lru_wx, lru_bx, lru_lambda, od_w_in, od_w_out, sgu_norm, sgu_w, sgu_b, router_w, router_b, moe_w_gu, moe_b_gu, moe_w_down, moe_b_down):
    bp, sp_len, d = x_prompt.shape
    bs, ss_len, _ = x_sample.shape
    depth = w_ada.shape[0]
    heads, dh = cache_na_k.shape[3], cache_na_k.shape[4]
    na_w = heads * dh
    lru_w = state_lru.shape[-1]
    n_exp = router_w.shape[-1]
    ns, npr = bs * ss_len, bp * sp_len
    n = ns + npr
    assert bs == REC_SEQS and bp % REC_SEQS == 0
    assert ss_len % ROW_TILE == 0 and sp_len % ROW_TILE == 0

    x = jnp.concatenate([x_sample.reshape(ns, d), x_prompt.reshape(npr, d)], axis=0)
    tile_cond = jnp.asarray(
        np.concatenate([1 + np.arange(ns // ROW_TILE) // (ss_len // ROW_TILE),
                        np.zeros(npr // ROW_TILE, np.int64)]).astype(np.int32))
    cond_rows = 16
    cond = jnp.concatenate([c_ctx[None, :], c, jnp.zeros((cond_rows - 1 - bs, d), F32)], axis=0)
    mod = _adaln(cond, w_ada, b_ada)
    mod = mod.reshape(depth, cond_rows, 6, d).transpose(0, 2, 1, 3).reshape(depth, 6, cond_rows, 1, d)

    new_k = new_v = new_h = None
    for layer in range(depth):
        sm, scm, gm, sf, scf, gf = [mod[layer, i] for i in range(6)]
        g_mix = norm_mix[layer][None, :]
        g_ffn = norm_ffn[layer][None, :]
        if layer % 2 == 0:
            e = layer // 2
            ev_in = ev_w_in.shape[-1]
            head_gain = jnp.concatenate([jnp.tile(na_q_norm[e], heads), jnp.tile(na_k_norm[e], heads),
                                         jnp.ones((ev_in - 2 * na_w,), F32)])[None, :]
            proj = _inproj(x, tile_cond, g_mix, scm, sm, ev_w_in[e].astype(BF16), head_gain,
                           mode="heads", n_norm_cols=2 * na_w)
            att_s = _na_attention(proj, cache_na_k[:, e].reshape(bs, -1, na_w),
                                  cache_na_v[:, e].reshape(bs, -1, na_w), _na_bias_table(na_rpb[e]),
                                  nseq=bs, seq=ss_len, heads=heads, dh=dh)
            att_p = _ctx_attention(proj, row0=ns, nseq=bp, seq=sp_len, heads=heads, dh=dh)
            wgate, bgate = _pack_gate_weights(lru_wa[e], lru_ba[e], lru_wx[e], lru_bx[e])
            rec_args = (lru_conv_w[e], lru_conv_b[e][None, :], wgate, bgate, lru_lambda[e])
            h0_s = state_lru[:, e].transpose(1, 0, 2)[None]
            rec_s, _ = _recurrent(proj, *rec_args, h0_s, row0=0, nseq=bs, seq=ss_len,
                                  xr_col=3 * na_w, yg_col=3 * na_w + lru_w, width=lru_w)
            h0_p = jnp.zeros((bp // REC_SEQS, 2, REC_SEQS, lru_w), F32)
            rec_p, hl = _recurrent(proj, *rec_args, h0_p, row0=ns, nseq=bp, seq=sp_len,
                                   xr_col=3 * na_w, yg_col=3 * na_w + lru_w, width=lru_w)
            new_k = proj[ns:, na_w:2 * na_w].reshape(bp, 1, sp_len, heads, dh)
            new_v = proj[ns:, 2 * na_w:3 * na_w].reshape(bp, 1, sp_len, heads, dh)
            new_h = hl.transpose(0, 2, 1, 3).reshape(bp, 1, 2, lru_w)
            att = jnp.concatenate([att_s, att_p], axis=0)
            rec = jnp.concatenate([rec_s, rec_p], axis=0)
            w_out = ev_w_out[e].astype(BF16)
            lhs_list, w_list = [att, rec], [w_out[:na_w], w_out[na_w:]]
        else:
            o = layer // 2
            z = _inproj(x, tile_cond, g_mix, scm, sm, od_w_in[o].astype(BF16),
                        jnp.ones((1, od_w_in.shape[-1]), F32), mode="gelu", n_norm_cols=0)
            chunk = sgu_w.shape[-1]
            sg = _sgu(z, sgu_norm[o][None, :], sgu_w[o].astype(BF16), sgu_b[o].T, chunk=chunk)
            lhs_list, w_list = [sg], [od_w_out[o].astype(BF16)]
        rw = jnp.pad(router_w[layer], ((0, 0), (0, LANES - n_exp)))
        rw_hi, rw_lo = _split_bf16(rw)
        rb = jnp.pad(router_b[layer], (0, LANES - n_exp))[None, :]
        x, h2, idx, wts = _outproj(lhs_list, w_list, x, tile_cond, gm, g_ffn, scf, sf,
                                   rw_hi, rw_lo, rb, n_experts=n_exp)
        x = _moe(x, h2, idx, wts, tile_cond, gf, moe_w_gu, moe_b_gu, moe_w_down, moe_b_down, layer)

    y_sample = x[:ns].reshape(bs, ss_len, d)
    y_prompt = x[ns:].reshape(bp, sp_len, d)
    new_na_k = new_k
    new_na_v = new_v
    new_state_lru = new_h
    return (y_prompt, y_sample, new_na_k, new_na_v, new_state_lru)
```

```python
import functools

import numpy as np
import jax
import jax.numpy as jnp
from jax import lax
from jax.experimental import pallas as pl
from jax.experimental.pallas import tpu as pltpu

F32 = jnp.float32
BF16 = jnp.bfloat16

EPS = 1e-6
NEG_INF = -1e30
GRID_W = 64
NA_KH = 8
NA_KW = 16
LRU_C = 8.0
SWIGLU_ALPHA = 1.702
SWIGLU_LIMIT = 7.0
TOP_K = 4

LANES = 128
ROW_TILE = 512
TOKEN_TILE = 256
MOE_SUB = 256
MOE_ITEM_SUBS = 7
MOE_TF = 256
MOE_TN = 512
VMEM_LIMIT = 56 * 1024 * 1024


def _cparams(sem, vmem=VMEM_LIMIT):
    return pltpu.CompilerParams(dimension_semantics=sem, vmem_limit_bytes=vmem)


def _split_bf16(x):
    hi = x.astype(BF16)
    lo = (x - hi.astype(F32)).astype(BF16)
    return hi, lo


def _gelu_tanh(x):
    return 0.5 * x * (1.0 + jnp.tanh(0.7978845608028654 * (x + 0.044715 * x * x * x)))


def _dot(a, b):
    return jnp.dot(a, b, preferred_element_type=F32)


def _dot_nt(a, b):
    return lax.dot_general(a, b, (((1,), (1,)), ((), ())), preferred_element_type=F32)


def _adaln_kernel(c_ref, w_ref, b_ref, o_ref):
    cnd = c_ref[...]
    s = cnd * jax.nn.sigmoid(cnd)
    hi, lo = _split_bf16(s)
    w = w_ref[...].astype(BF16)
    o_ref[...] = _dot(hi, w) + _dot(lo, w) + b_ref[...]


def _adaln(cond, w_ada, b_ada, tn=1024):
    depth, d, n6 = w_ada.shape
    rows = cond.shape[0]
    return pl.pallas_call(
        _adaln_kernel,
        grid=(depth, n6 // tn),
        in_specs=[
            pl.BlockSpec((rows, d), lambda l, j: (0, 0)),
            pl.BlockSpec((None, d, tn), lambda l, j: (l, 0, j)),
            pl.BlockSpec((None, 1, tn), lambda l, j: (l, 0, j)),
        ],
        out_specs=pl.BlockSpec((None, rows, tn), lambda l, j: (l, 0, j)),
        out_shape=jax.ShapeDtypeStruct((depth, rows, n6), F32),
        compiler_params=_cparams(("arbitrary", "arbitrary")),
        name="adaln",
    )(cond, w_ada, b_ada.reshape(depth, 1, n6))


def _modulated_norm(x, g, sc, sh):
    ms = jnp.mean(x * x, axis=-1, keepdims=True)
    return (x * lax.rsqrt(ms + EPS) * g) * (1.0 + sc) + sh


def _inproj_kernel(tc_ref, x_ref, g_ref, sc_ref, sh_ref, w_ref, hg_ref, o_ref, h_scr, *,
                   mode, n_norm_tiles):
    j = pl.program_id(1)

    @pl.when(j == 0)
    def _():
        h_scr[...] = _modulated_norm(x_ref[...], g_ref[...], sc_ref[...], sh_ref[...]).astype(BF16)

    acc = _dot(h_scr[...], w_ref[...])
    if mode == "gelu":
        o_ref[...] = _gelu_tanh(acc)
    else:
        @pl.when(j < n_norm_tiles)
        def _():
            hg = hg_ref[...]
            for c in range(acc.shape[1] // LANES):
                blk = acc[:, c * LANES:(c + 1) * LANES]
                ms = jnp.mean(blk * blk, axis=-1, keepdims=True)
                o_ref[:, c * LANES:(c + 1) * LANES] = (
                    blk * lax.rsqrt(ms + EPS) * hg[:, c * LANES:(c + 1) * LANES])

        @pl.when(j >= n_norm_tiles)
        def _():
            o_ref[...] = acc


def _inproj(x, tile_cond, norm_g, sc, sh, w_bf16, head_gain, *, mode, n_norm_cols, tn=1024):
    n, d = x.shape
    nout = w_bf16.shape[1]
    tm = ROW_TILE
    kern = functools.partial(_inproj_kernel, mode=mode, n_norm_tiles=n_norm_cols // tn)
    modspec = pl.BlockSpec((None, 1, d), lambda i, j, tc: (tc[i], 0, 0))
    return pl.pallas_call(
        kern,
        grid_spec=pltpu.PrefetchScalarGridSpec(
            num_scalar_prefetch=1,
            grid=(n // tm, nout // tn),
            in_specs=[
                pl.BlockSpec((tm, d), lambda i, j, tc: (i, 0)),
                pl.BlockSpec((1, d), lambda i, j, tc: (0, 0)),
                modspec, modspec,
                pl.BlockSpec((d, tn), lambda i, j, tc: (0, j)),
                pl.BlockSpec((1, tn), lambda i, j, tc: (0, j)),
            ],
            out_specs=pl.BlockSpec((tm, tn), lambda i, j, tc: (i, j)),
            scratch_shapes=[pltpu.VMEM((tm, d), BF16)],
        ),
        out_shape=jax.ShapeDtypeStruct((n, nout), F32),
        compiler_params=_cparams(("arbitrary", "arbitrary")),
        name="inproj_" + mode,
    )(tile_cond, x, norm_g, sc, sh, w_bf16, head_gain)


def _ctx_attn_kernel(q_ref, k_ref, v_ref, o_ref, *, heads, dh):
    scale = dh ** -0.5
    for h in range(heads):
        sl = slice(h * dh, (h + 1) * dh)
        q = q_ref[:, sl].astype(BF16)
        k = k_ref[:, sl].astype(BF16)
        v = v_ref[:, sl].astype(BF16)
        s = _dot_nt(q, k) * scale
        m = jnp.max(s, axis=-1, keepdims=True)
        p = jnp.exp(s - m)
        den = jnp.sum(p, axis=-1, keepdims=True)
        o = _dot(p.astype(BF16), v) / den
        o_ref[:, sl] = o.astype(o_ref.dtype)


def _ctx_attention(proj, *, row0, nseq, seq, heads, dh):
    width = heads * dh
    rb0 = row0 // seq
    kern = functools.partial(_ctx_attn_kernel, heads=heads, dh=dh)
    return pl.pallas_call(
        kern,
        grid=(nseq,),
        in_specs=[
            pl.BlockSpec((seq, width), lambda b: (rb0 + b, 0)),
            pl.BlockSpec((seq, width), lambda b: (rb0 + b, 1)),
            pl.BlockSpec((seq, width), lambda b: (rb0 + b, 2)),
        ],
        out_specs=pl.BlockSpec((seq, width), lambda b: (b, 0)),
        out_shape=jax.ShapeDtypeStruct((nseq * seq, width), BF16),
        compiler_params=_cparams(("arbitrary",)),
        name="ctx_attention",
    )(proj, proj, proj)


def _na_attn_kernel(q_ref, k_ref, v_ref, kc_ref, vc_ref, bias_ref, o_ref, *, rows, dh):
    scale = dh ** -0.5
    win = NA_KH * GRID_W
    qc = lax.broadcasted_iota(jnp.int32, (GRID_W, win), 0)
    kc = lax.broadcasted_iota(jnp.int32, (GRID_W, win), 1) % GRID_W
    c_start = jnp.clip(qc - NA_KW // 2, 0, GRID_W - NA_KW)
    col_ok = (kc >= c_start) & (kc < c_start + NA_KW)
    k_ctx = kc_ref[...].astype(BF16)
    v_ctx = vc_ref[...].astype(BF16)
    for r in range(rows):
        rs = min(max(r - NA_KH // 2, 0), rows - NA_KH)
        q = q_ref[r * GRID_W:(r + 1) * GRID_W, :].astype(BF16)
        kw = k_ref[rs * GRID_W:rs * GRID_W + win, :].astype(BF16)
        vw = v_ref[rs * GRID_W:rs * GRID_W + win, :].astype(BF16)
        s_win = _dot_nt(q, kw) * scale + bias_ref[r - rs]
        s_win = jnp.where(col_ok, s_win, NEG_INF)
        s_ctx = _dot_nt(q, k_ctx) * scale
        m = jnp.maximum(jnp.max(s_win, axis=-1, keepdims=True),
                        jnp.max(s_ctx, axis=-1, keepdims=True))
        p_win = jnp.exp(s_win - m)
        p_ctx = jnp.exp(s_ctx - m)
        den = jnp.sum(p_win, axis=-1, keepdims=True) + jnp.sum(p_ctx, axis=-1, keepdims=True)
        o = (_dot(p_win.astype(BF16), vw) + _dot(p_ctx.astype(BF16), v_ctx)) / den
        o_ref[r * GRID_W:(r + 1) * GRID_W, :] = o.astype(o_ref.dtype)


def _na_bias_kernel(rpb_ref, o_ref, *, n_rows, n_cols):
    h = pl.program_id(0)
    q = lax.broadcasted_iota(jnp.int32, (GRID_W, GRID_W), 0)
    k = lax.broadcasted_iota(jnp.int32, (GRID_W, GRID_W), 1)
    col_idx = jnp.clip(k - q + NA_KW - 1, 0, n_cols - 1)
    for ri in range(n_rows):
        acc = jnp.zeros((GRID_W, GRID_W), F32)
        for j in range(n_cols):
            acc = jnp.where(col_idx == j, rpb_ref[(h * n_rows + ri) * n_cols + j], acc)
        o_ref[ri] = acc


def _na_bias_table(rpb):
    heads, n_rows, n_cols = rpb.shape
    toep = pl.pallas_call(
        functools.partial(_na_bias_kernel, n_rows=n_rows, n_cols=n_cols),
        grid=(heads,),
        in_specs=[pl.BlockSpec(memory_space=pltpu.SMEM)],
        out_specs=pl.BlockSpec((None, n_rows, GRID_W, GRID_W), lambda h: (h, 0, 0, 0)),
        out_shape=jax.ShapeDtypeStruct((heads, n_rows, GRID_W, GRID_W), F32),
        compiler_params=_cparams(("arbitrary",)),
        name="na_bias",
    )(rpb.reshape(-1))
    classes = []
    for d in range(NA_KH):
        t = toep[:, NA_KH - 1 - d:2 * NA_KH - 1 - d]
        classes.append(t.transpose(0, 2, 1, 3).reshape(heads, GRID_W, NA_KH * GRID_W))
    return jnp.stack(classes, axis=1)


def _na_attention(proj, cache_k, cache_v, bias_tab, *, nseq, seq, heads, dh):
    past = cache_k.shape[1]
    rows = seq // GRID_W
    kern = functools.partial(_na_attn_kernel, rows=rows, dh=dh)
    return pl.pallas_call(
        kern,
        grid=(heads, nseq),
        in_specs=[
            pl.BlockSpec((seq, dh), lambda h, b: (b, h)),
            pl.BlockSpec((seq, dh), lambda h, b: (b, heads + h)),
            pl.BlockSpec((seq, dh), lambda h, b: (b, 2 * heads + h)),
            pl.BlockSpec((None, past, dh), lambda h, b: (b, 0, h)),
            pl.BlockSpec((None, past, dh), lambda h, b: (b, 0, h)),
            pl.BlockSpec((None, NA_KH, GRID_W, NA_KH * GRID_W), lambda h, b: (h, 0, 0, 0)),
        ],
        out_specs=pl.BlockSpec((seq, dh), lambda h, b: (b, h)),
        out_shape=jax.ShapeDtypeStruct((nseq * seq, heads * dh), BF16),
        compiler_params=_cparams(("arbitrary", "arbitrary")),
        name="na_attention",
    )(proj, proj, proj, cache_k, cache_v, bias_tab)


REC_SEQS = 8
REC_CHUNK = 256


def _rec_kernel(xr_ref, yg_ref, cw_ref, cb_ref, wg_ref, bg_ref, lam_ref, h0_ref,
                out_ref, hl_ref, a_scr, b_scr, *, seq):
    pitch = seq + 8
    total = REC_SEQS * seq
    nchunks = total // REC_CHUNK
    row = lax.broadcasted_iota(jnp.int32, (REC_CHUNK, LANES), 0)
    cw = cw_ref[...]
    cb = cb_ref[...]
    wg_hi = wg_ref[0]
    wg_lo = wg_ref[1]
    bg = bg_ref[...]
    lam = lam_ref[...]
    sp = jnp.maximum(-lam, 0.0) + jnp.log(1.0 + jnp.exp(-jnp.abs(lam)))

    def coeff_chunk(ci, carry):
        r0 = pl.multiple_of(ci * REC_CHUNK, REC_CHUNK)
        t0 = r0 % seq
        x = xr_ref[pl.ds(r0, REC_CHUNK), :]
        prev = xr_ref[pl.ds(pl.multiple_of(jnp.maximum(r0 - 8, 0), 8), 8), :]
        nxt = xr_ref[pl.ds(pl.multiple_of(jnp.minimum(r0 + REC_CHUNK, total - 8), 8), 8), :]
        has_prev = (t0 > 0).astype(F32)
        has_next = (t0 + REC_CHUNK < seq).astype(F32)
        p6 = prev[6:7, :] * has_prev
        p7 = prev[7:8, :] * has_prev
        n0 = nxt[0:1, :] * has_next
        x_m1 = jnp.where(row == 0, p7, pltpu.roll(x, 1, axis=0))
        x_m2 = jnp.where(row == 0, p6, jnp.where(row == 1, p7, pltpu.roll(x, 2, axis=0)))
        x_p1 = jnp.where(row == REC_CHUNK - 1, n0, pltpu.roll(x, REC_CHUNK - 1, axis=0))
        xc = cb + x_m2 * cw[0:1, :] + x_m1 * cw[1:2, :] + x * cw[2:3, :] + x_p1 * cw[3:4, :]
        xh, xl = _split_bf16(xc)
        gates = _dot(xh, wg_hi) + _dot(xl, wg_hi) + _dot(xh, wg_lo) + bg
        srow = pl.multiple_of((r0 // seq) * pitch + t0, 8)
        for d in range(2):
            r = jax.nn.sigmoid(gates[:, (2 * d) * LANES:(2 * d + 1) * LANES])
            i = jax.nn.sigmoid(gates[:, (2 * d + 1) * LANES:(2 * d + 2) * LANES])
            log_a = (-LRU_C) * r * sp[d:d + 1, :]
            a = jnp.exp(log_a)
            b = jnp.sqrt(1.0 - jnp.exp(2.0 * log_a)) * (i * xc)
            a_scr[d, pl.ds(srow, REC_CHUNK), :] = a
            b_scr[d, pl.ds(srow, REC_CHUNK), :] = b
        return carry

    lax.fori_loop(0, nchunks, coeff_chunk, 0)

    def scan_step(t, carry):
        hf, hb = carry
        tb = seq - 1 - t
        hf = a_scr[0, pl.ds(t, REC_SEQS, stride=pitch), :] * hf + b_scr[0, pl.ds(t, REC_SEQS, stride=pitch), :]
        b_scr[0, pl.ds(t, REC_SEQS, stride=pitch), :] = hf
        hb = a_scr[1, pl.ds(tb, REC_SEQS, stride=pitch), :] * hb + b_scr[1, pl.ds(tb, REC_SEQS, stride=pitch), :]
        b_scr[1, pl.ds(tb, REC_SEQS, stride=pitch), :] = hb
        return hf, hb

    hf, hb = lax.fori_loop(0, seq, scan_step, (h0_ref[0], h0_ref[1]), unroll=8)
    hl_ref[0] = hf
    hl_ref[1] = hb

    def out_chunk(ci, carry):
        r0 = pl.multiple_of(ci * REC_CHUNK, REC_CHUNK)
        srow = pl.multiple_of((r0 // seq) * pitch + r0 % seq, 8)
        hsum = b_scr[0, pl.ds(srow, REC_CHUNK), :] + b_scr[1, pl.ds(srow, REC_CHUNK), :]
        out_ref[pl.ds(r0, REC_CHUNK), :] = (_gelu_tanh(yg_ref[pl.ds(r0, REC_CHUNK), :]) * hsum).astype(out_ref.dtype)
        return carry

    lax.fori_loop(0, nchunks, out_chunk, 0)


def _recurrent(proj, conv_w, conv_b, wgate, bgate, lam, h0, *, row0, nseq, seq, xr_col, yg_col, width):
    groups = nseq // REC_SEQS
    blk_rows = REC_SEQS * seq
    rb0 = row0 // blk_rows
    nblk = width // LANES
    xc0 = xr_col // LANES
    yc0 = yg_col // LANES
    pitch = seq + 8
    kern = functools.partial(_rec_kernel, seq=seq)
    return pl.pallas_call(
        kern,
        grid=(groups, nblk),
        in_specs=[
            pl.BlockSpec((blk_rows, LANES), lambda g, n: (rb0 + g, xc0 + n)),
            pl.BlockSpec((blk_rows, LANES), lambda g, n: (rb0 + g, yc0 + n)),
            pl.BlockSpec((4, LANES), lambda g, n: (0, n)),
            pl.BlockSpec((1, LANES), lambda g, n: (0, n)),
            pl.BlockSpec((None, 2, LANES, 4 * LANES), lambda g, n: (n, 0, 0, 0)),
            pl.BlockSpec((None, 1, 4 * LANES), lambda g, n: (n, 0, 0)),
            pl.BlockSpec((2, LANES), lambda g, n: (0, n)),
            pl.BlockSpec((None, 2, REC_SEQS, LANES), lambda g, n: (g, 0, 0, n)),
        ],
        out_specs=[
            pl.BlockSpec((blk_rows, LANES), lambda g, n: (g, n)),
            pl.BlockSpec((None, 2, REC_SEQS, LANES), lambda g, n: (g, 0, 0, n)),
        ],
        out_shape=[
            jax.ShapeDtypeStruct((nseq * seq, width), BF16),
            jax.ShapeDtypeStruct((groups, 2, REC_SEQS, width), F32),
        ],
        scratch_shapes=[
            pltpu.VMEM((2, REC_SEQS * pitch, LANES), F32),
            pltpu.VMEM((2, REC_SEQS * pitch, LANES), F32),
        ],
        compiler_params=_cparams(("arbitrary", "arbitrary")),
        name="recurrent",
    )(proj, proj, conv_w, conv_b, wgate, bgate, lam, h0)


def _pack_gate_weights(wa, ba, wx, bx):
    nblk = wa.shape[1]
    w = jnp.concatenate([wa[0], wx[0], wa[1], wx[1]], axis=-1)
    hi, lo = _split_bf16(w)
    b = jnp.stack([ba[0].reshape(nblk, LANES), bx[0].reshape(nblk, LANES),
                   ba[1].reshape(nblk, LANES), bx[1].reshape(nblk, LANES)], axis=1)
    return jnp.stack([hi, lo], axis=1), b.reshape(nblk, 1, 4 * LANES)


def _sgu_kernel(u_ref, v_ref, g_ref, ws_ref, bs_ref, o_ref, *, chunk, groups):
    g = g_ref[...]
    for c in range(u_ref.shape[0] // chunk):
        rs = slice(c * chunk, (c + 1) * chunk)
        v = v_ref[rs, :]
        ms = jnp.mean(v * v, axis=-1, keepdims=True)
        vn = (v * lax.rsqrt(ms + EPS) * g).astype(BF16)
        for grp in range(groups):
            cs = slice(grp * LANES, (grp + 1) * LANES)
            mixed = _dot(ws_ref[grp], vn[:, cs]) + bs_ref[:, grp:grp + 1]
            o_ref[rs, cs] = (u_ref[rs, cs] * mixed).astype(o_ref.dtype)


def _sgu(z, v_norm, ws_bf16, bs_t, *, chunk, tm=512):
    n, w2 = z.shape
    w = w2 // 2
    groups = ws_bf16.shape[0]
    kern = functools.partial(_sgu_kernel, chunk=chunk, groups=groups)
    return pl.pallas_call(
        kern,
        grid=(n // tm,),
        in_specs=[
            pl.BlockSpec((tm, w), lambda i: (i, 0)),
            pl.BlockSpec((tm, w), lambda i: (i, 1)),
            pl.BlockSpec((1, w), lambda i: (0, 0)),
            pl.BlockSpec((groups, chunk, chunk), lambda i: (0, 0, 0)),
            pl.BlockSpec((chunk, groups), lambda i: (0, 0)),
        ],
        out_specs=pl.BlockSpec((tm, w), lambda i: (i, 0)),
        out_shape=jax.ShapeDtypeStruct((n, w), BF16),
        compiler_params=_cparams(("arbitrary",)),
        name="sgu",
    )(z, z, v_norm, ws_bf16, bs_t)


def _outproj_kernel(tc_ref, *refs, n_lhs, n_experts):
    lhs = refs[:n_lhs]
    ws = refs[n_lhs:2 * n_lhs]
    (x_ref, gm_ref, g2_ref, scf_ref, sf_ref, rwh_ref, rwl_ref, rb_ref, tri_ref,
     xo_ref, h2_ref, meta_ref, wts_ref, cnt_ref, carry) = refs[2 * n_lhs:]
    half = h2_ref.shape[1]

    @pl.when(pl.program_id(0) == 0)
    def _():
        carry[...] = jnp.zeros(carry.shape, carry.dtype)

    mix = _dot(lhs[0][...], ws[0][...])
    for a_ref, w_ref in zip(lhs[1:], ws[1:]):
        mix = mix + _dot(a_ref[...], w_ref[...])
    xn = x_ref[...] + gm_ref[...] * mix
    xo_ref[...] = xn
    h2 = _modulated_norm(xn, g2_ref[...], scf_ref[...], sf_ref[...])
    hi, lo = _split_bf16(h2)
    bits = pltpu.bitcast(hi.astype(F32), jnp.uint32)
    h2_ref[...] = (bits[:, :half] >> 16) | (bits[:, half:] & jnp.uint32(0xFFFF0000))
    logits = _dot(hi, rwh_ref[...]) + _dot(lo, rwh_ref[...]) + _dot(hi, rwl_ref[...]) + rb_ref[...]
    lane = lax.broadcasted_iota(jnp.int32, logits.shape, 1)
    l = jnp.where(lane < n_experts, logits, -jnp.inf)
    vals, idxs = [], []
    for _ in range(TOP_K):
        m = jnp.max(l, axis=-1, keepdims=True)
        i = jnp.min(jnp.where(l == m, lane, LANES), axis=-1, keepdims=True)
        vals.append(m)
        idxs.append(i)
        l = jnp.where(lane == i, -jnp.inf, l)
    es = [jnp.exp(v - vals[0]) for v in vals]
    den = es[0] + es[1] + es[2] + es[3]
    chosen = jnp.zeros(logits.shape, F32)
    for k in range(TOP_K):
        chosen = chosen + (lane == idxs[k]).astype(F32)
    before = carry[...] + _dot(tri_ref[...], chosen.astype(BF16))
    meta = jnp.zeros(logits.shape, jnp.int32)
    wts_out = jnp.zeros(logits.shape, F32)
    for k in range(TOP_K):
        rank = jnp.sum(jnp.where(lane == idxs[k], before, 0.0), axis=-1, keepdims=True).astype(jnp.int32)
        meta = jnp.where(lane == k, idxs[k], meta)
        meta = jnp.where(lane == TOP_K + k, rank, meta)
        wts_out = jnp.where(lane == k, es[k] / den, wts_out)
    meta_ref[...] = meta
    wts_ref[...] = wts_out
    total = carry[...] + jnp.sum(chosen, axis=0, keepdims=True)
    carry[...] = total
    cnt_ref[...] = total


def _outproj(lhs_list, w_list, x, tile_cond, gm, g2, scf, sf, rw_hi, rw_lo, rb, *, n_experts):
    n, d = x.shape
    tm = ROW_TILE
    n_lhs = len(lhs_list)
    kern = functools.partial(_outproj_kernel, n_lhs=n_lhs, n_experts=n_experts)
    modspec = pl.BlockSpec((None, 1, d), lambda i, tc: (tc[i], 0, 0))
    tri = jnp.asarray(np.tril(np.ones((tm, tm), np.float32), -1), BF16)
    in_specs = (
        [pl.BlockSpec((tm, a.shape[1]), lambda i, tc: (i, 0)) for a in lhs_list]
        + [pl.BlockSpec(w.shape, lambda i, tc: (0, 0)) for w in w_list]
        + [pl.BlockSpec((tm, d), lambda i, tc: (i, 0)),
           modspec,
           pl.BlockSpec((1, d), lambda i, tc: (0, 0)),
           modspec, modspec,
           pl.BlockSpec((d, LANES), lambda i, tc: (0, 0)),
           pl.BlockSpec((d, LANES), lambda i, tc: (0, 0)),
           pl.BlockSpec((1, LANES), lambda i, tc: (0, 0)),
           pl.BlockSpec((tm, tm), lambda i, tc: (0, 0))])
    return pl.pallas_call(
        kern,
        grid_spec=pltpu.PrefetchScalarGridSpec(
            num_scalar_prefetch=1,
            grid=(n // tm,),
            in_specs=in_specs,
            out_specs=[
                pl.BlockSpec((tm, d), lambda i, tc: (i, 0)),
                pl.BlockSpec((tm, d // 2), lambda i, tc: (i, 0)),
                pl.BlockSpec((tm, LANES), lambda i, tc: (i, 0)),
                pl.BlockSpec((tm, LANES), lambda i, tc: (i, 0)),
                pl.BlockSpec((1, LANES), lambda i, tc: (0, 0)),
            ],
            scratch_shapes=[pltpu.VMEM((1, LANES), F32)],
        ),
        out_shape=[
            jax.ShapeDtypeStruct((n, d), F32),
            jax.ShapeDtypeStruct((n, d // 2), jnp.uint32),
            jax.ShapeDtypeStruct((n, LANES), jnp.int32),
            jax.ShapeDtypeStruct((n, LANES), F32),
            jax.ShapeDtypeStruct((1, LANES), F32),
        ],
        compiler_params=_cparams(("arbitrary",)),
        name="outproj_router",
    )(tile_cond, *lhs_list, *w_list, x, gm, g2, scf, sf, rw_hi, rw_lo, rb, tri)


def _dispatch_kernel(pos_ref, h_ref, xs_in, xs_hbm, sem):
    del xs_in
    rows = h_ref.shape[0]

    def issue(r, carry):
        for k in range(TOP_K):
            p = pos_ref[0, r * TOP_K + k]
            pltpu.make_async_copy(h_ref.at[pl.ds(r, 1), :], xs_hbm.at[pl.ds(p, 1), :], sem).start()
        return carry

    lax.fori_loop(0, rows, issue, 0, unroll=4)
    for k in range(TOP_K):
        pltpu.make_async_copy(h_ref, xs_hbm.at[pl.ds(0, rows), :], sem).wait()


def _dispatch(h2p, pos_tiles, rows_padded):
    n, half = h2p.shape
    tm = TOKEN_TILE
    xs0 = jnp.zeros((rows_padded, half), h2p.dtype)
    return pl.pallas_call(
        _dispatch_kernel,
        grid=(n // tm,),
        in_specs=[
            pl.BlockSpec((None, 1, tm * TOP_K), lambda i: (i, 0, 0), memory_space=pltpu.SMEM),
            pl.BlockSpec((tm, half), lambda i: (i, 0)),
            pl.BlockSpec(memory_space=pl.ANY),
        ],
        out_specs=pl.BlockSpec(memory_space=pl.ANY),
        out_shape=jax.ShapeDtypeStruct((rows_padded, half), h2p.dtype),
        scratch_shapes=[pltpu.SemaphoreType.DMA],
        input_output_aliases={2: 0},
        compiler_params=_cparams(("arbitrary",)),
        name="moe_dispatch",
    )(pos_tiles, h2p, xs0)


def _moe_kernel(ie_ref, it0_ref, int_ref, inz_ref, xs_hbm, wg_ref, wl_ref, wd_ref, bg_ref, bl_ref, bd_ref,
                ys_hbm, stage, xb, act, out, wgb, wlb, wdb, sem_in, sem_out, *, n1, n2):
    s = pl.program_id(0)
    c = pl.program_id(1)
    nt = int_ref[s]
    nz = inz_ref[s]
    t0 = it0_ref[s]
    half = stage.shape[2]

    def copy_in(j):
        return pltpu.make_async_copy(
            xs_hbm.at[pl.ds((t0 + j) * MOE_SUB, MOE_SUB), :], stage.at[j % 2], sem_in.at[j % 2])

    def copy_out(n, j):
        col = n * MOE_TN if isinstance(n, int) else pl.multiple_of(n * MOE_TN, MOE_TN)
        return pltpu.make_async_copy(
            out.at[n, pl.ds(j * MOE_SUB, MOE_SUB), :],
            ys_hbm.at[pl.ds((t0 + j) * MOE_SUB, MOE_SUB), pl.ds(col, MOE_TN)],
            sem_out.at[n])

    def sub_tiles(fn):
        def pair(p, carry):
            fn(2 * p)
            fn(2 * p + 1)
            return carry

        lax.fori_loop(0, nt // 2, pair, 0)

        @pl.when(nt % 2 == 1)
        def _():
            fn(nt - 1)

    @pl.when((c == 0) & (nz > 0))
    def _():
        out[0, pl.ds(0, MOE_SUB), :] = jnp.zeros((MOE_SUB, MOE_TN), out.dtype)

        def copy_zero(n, j):
            return pltpu.make_async_copy(
                out.at[0, pl.ds(0, MOE_SUB), :],
                ys_hbm.at[pl.ds((t0 + j) * MOE_SUB, MOE_SUB), pl.ds(n * MOE_TN, MOE_TN)], sem_out.at[n])

        for j in range(MOE_ITEM_SUBS):
            @pl.when(j < nz)
            def _():
                for n in range(n2):
                    copy_zero(n, j).start()
        for j in range(MOE_ITEM_SUBS):
            @pl.when(j < nz)
            def _():
                for n in range(n2):
                    copy_zero(n, j).wait()

    @pl.when((c == 0) & (nt > 0))
    def _():
        copy_in(0).start()
        for j in range(MOE_ITEM_SUBS):
            @pl.when(j < nt)
            def _():
                if j + 1 < MOE_ITEM_SUBS:
                    @pl.when(j + 1 < nt)
                    def _():
                        copy_in(j + 1).start()
                copy_in(j).wait()
                w = stage[j % 2]
                rows = slice(j * MOE_SUB, (j + 1) * MOE_SUB)
                xb[rows, :half] = pltpu.bitcast(w << 16, F32).astype(BF16)
                xb[rows, half:] = pltpu.bitcast(w & jnp.uint32(0xFFFF0000), F32).astype(BF16)

    @pl.when((nt > 0) & (c < n1))
    def _():
        wgb[...] = wg_ref[...].astype(BF16)
        wlb[...] = wl_ref[...].astype(BF16)
        bg = bg_ref[...]
        bl = bl_ref[...]
        cc = jnp.minimum(c, n1 - 1)

        def up(j):
            r0 = pl.multiple_of(j * MOE_SUB, MOE_SUB)
            x = xb[pl.ds(r0, MOE_SUB), :]
            glu = jnp.minimum(_dot(x, wgb[...]) + bg, SWIGLU_LIMIT)
            lin = jnp.clip(_dot(x, wlb[...]) + bl, -SWIGLU_LIMIT, SWIGLU_LIMIT)
            act[cc, pl.ds(r0, MOE_SUB), :] = (glu * jax.nn.sigmoid(SWIGLU_ALPHA * glu) * (lin + 1.0)).astype(BF16)

        sub_tiles(up)

    @pl.when((nt > 0) & (c >= n1))
    def _():
        wdb[...] = wd_ref[...].astype(BF16)
        bd = bd_ref[...]
        n = jnp.maximum(c - n1, 0)

        def down(j):
            r0 = pl.multiple_of(j * MOE_SUB, MOE_SUB)
            a = jnp.concatenate([act[k, pl.ds(r0, MOE_SUB), :] for k in range(n1)], axis=1)
            out[n, pl.ds(r0, MOE_SUB), :] = _dot(a, wdb[...]) + bd

        sub_tiles(down)
        for j in range(MOE_ITEM_SUBS):
            @pl.when(j < nt)
            def _():
                copy_out(n, j).start()

    @pl.when((nt > 0) & (c == n1 + n2 - 1))
    def _():
        for n in range(n2):
            for j in range(MOE_ITEM_SUBS):
                @pl.when(j < nt)
                def _():
                    copy_out(n, j).wait()


def _moe_experts(xs, items, w_gu, b_gu, w_down, b_down, layer):
    rp, half = xs.shape
    d = 2 * half
    _, n_exp, _, f2 = w_gu.shape
    f = f2 // 2
    n1 = f // MOE_TF
    n2 = d // MOE_TN
    item_e, item_t0, item_nt, item_nz = items
    n_items = item_e.shape[0]
    tm = MOE_ITEM_SUBS * MOE_SUB

    def up_idx(c, nt_s):
        return jnp.where(nt_s > 0, jnp.minimum(c, n1 - 1), n1 - 1)

    def down_idx(c, nt_s):
        return jnp.where(nt_s > 0, jnp.maximum(c - n1, 0), n2 - 1)

    kern = functools.partial(_moe_kernel, n1=n1, n2=n2)
    return pl.pallas_call(
        kern,
        grid_spec=pltpu.PrefetchScalarGridSpec(
            num_scalar_prefetch=4,
            grid=(n_items, n1 + n2),
            in_specs=[
                pl.BlockSpec(memory_space=pl.ANY),
                pl.BlockSpec((None, None, d, MOE_TF),
                             lambda s, c, ie, it0, nt, nz: (layer, ie[s], 0, up_idx(c, nt[s]))),
                pl.BlockSpec((None, None, d, MOE_TF),
                             lambda s, c, ie, it0, nt, nz: (layer, ie[s], 0, n1 + up_idx(c, nt[s]))),
                pl.BlockSpec((None, None, f, MOE_TN),
                             lambda s, c, ie, it0, nt, nz: (layer, ie[s], 0, down_idx(c, nt[s]))),
                pl.BlockSpec((None, None, 1, MOE_TF),
                             lambda s, c, ie, it0, nt, nz: (layer, ie[s], 0, up_idx(c, nt[s]))),
                pl.BlockSpec((None, None, 1, MOE_TF),
                             lambda s, c, ie, it0, nt, nz: (layer, ie[s], 0, n1 + up_idx(c, nt[s]))),
                pl.BlockSpec((None, None, 1, MOE_TN),
                             lambda s, c, ie, it0, nt, nz: (layer, ie[s], 0, down_idx(c, nt[s]))),
            ],
            out_specs=pl.BlockSpec(memory_space=pl.ANY),
            scratch_shapes=[
                pltpu.VMEM((2, MOE_SUB, half), jnp.uint32),
                pltpu.VMEM((tm, d), BF16),
                pltpu.VMEM((n1, tm, MOE_TF), BF16),
                pltpu.VMEM((n2, tm, MOE_TN), F32),
                pltpu.VMEM((d, MOE_TF), BF16),
                pltpu.VMEM((d, MOE_TF), BF16),
                pltpu.VMEM((f, MOE_TN), BF16),
                pltpu.SemaphoreType.DMA((2,)),
                pltpu.SemaphoreType.DMA((n2,)),
            ],
        ),
        out_shape=jax.ShapeDtypeStruct((rp, d), F32),
        compiler_params=_cparams(("arbitrary", "arbitrary")),
        name="moe_experts",
    )(item_e, item_t0, item_nt, item_nz, xs, w_gu, w_gu, w_down,
      b_gu.reshape(b_gu.shape[0], n_exp, 1, f2), b_gu.reshape(b_gu.shape[0], n_exp, 1, f2),
      b_down.reshape(b_down.shape[0], n_exp, 1, d))


def _moe_plan(meta, counts, n_experts, rows_padded):
    n = meta.shape[0]
    top_i = meta[:, :TOP_K]
    rank = meta[:, TOP_K:2 * TOP_K]
    counts = counts[0, :n_experts].astype(jnp.int32)
    ntiles = (counts + MOE_SUB - 1) // MOE_SUB
    tile_end = jnp.cumsum(ntiles)
    tile_start = tile_end - ntiles
    experts = jnp.arange(n_experts, dtype=jnp.int32)
    start_sel = jnp.sum(jnp.where(top_i[:, :, None] == experts[None, None, :], tile_start[None, None, :], 0), axis=-1)
    pos = start_sel * MOE_SUB + rank
    n_items_max = n_experts + (n * TOP_K // MOE_SUB + MOE_ITEM_SUBS - 1) // MOE_ITEM_SUBS
    items_e = (ntiles + MOE_ITEM_SUBS - 1) // MOE_ITEM_SUBS
    item_end = jnp.cumsum(items_e)
    item_start = item_end - items_e
    sidx = jnp.arange(n_items_max, dtype=jnp.int32)
    valid = sidx < item_end[-1]
    ie = jnp.sum((sidx[:, None] >= item_end[None, :]).astype(jnp.int32), axis=1)
    ie = jnp.minimum(ie, n_experts - 1)
    pick = ie[:, None] == experts[None, :]
    local = sidx - jnp.sum(jnp.where(pick, item_start[None, :], 0), axis=1)
    t0 = jnp.sum(jnp.where(pick, tile_start[None, :], 0), axis=1) + local * MOE_ITEM_SUBS
    nt = jnp.clip(jnp.sum(jnp.where(pick, ntiles[None, :], 0), axis=1) - local * MOE_ITEM_SUBS, 0, MOE_ITEM_SUBS)
    last_e = jnp.max(jnp.where(valid, ie, 0))
    ie = jnp.where(valid, ie, last_e).astype(jnp.int32)
    z0 = tile_end[-1] + (sidx - item_end[-1]) * MOE_ITEM_SUBS
    nz = jnp.clip(rows_padded // MOE_SUB - z0, 0, MOE_ITEM_SUBS)
    t0 = jnp.where(valid, t0, z0).astype(jnp.int32)
    nt = jnp.where(valid, nt, 0).astype(jnp.int32)
    nz = jnp.where(valid, 0, nz).astype(jnp.int32)
    return pos.astype(jnp.int32), (ie, t0, nt, nz)


def _combine_kernel(tc_ref, pos_ref, posn_ref, x_ref, w_ref, gf_ref, ys_hbm, o_ref, buf, sem):
    i = pl.program_id(0)
    n_tiles = pl.num_programs(0)
    rows = x_ref.shape[0]

    def issue(pref, slot):
        def body(r, carry):
            for k in range(TOP_K):
                p = pref[0, r * TOP_K + k]
                pltpu.make_async_copy(ys_hbm.at[pl.ds(p, 1), :], buf.at[slot, k, pl.ds(r, 1), :],
                                      sem.at[slot]).start()
            return carry

        lax.fori_loop(0, rows, body, 0, unroll=4)

    slot = i % 2

    @pl.when(i == 0)
    def _():
        issue(pos_ref, 0)

    @pl.when(i + 1 < n_tiles)
    def _():
        issue(posn_ref, 1 - slot)

    for k in range(TOP_K):
        pltpu.make_async_copy(ys_hbm.at[pl.ds(0, rows), :], buf.at[slot, k], sem.at[slot]).wait()
    w = w_ref[...]
    acc = w[:, 0:1] * buf[slot, 0]
    for k in range(1, TOP_K):
        acc = acc + w[:, k:k + 1] * buf[slot, k]
    o_ref[...] = x_ref[...] + gf_ref[...] * acc


def _combine(x, ys, pos_tiles, wts, tile_cond, gf):
    n, d = x.shape
    tm = TOKEN_TILE
    n_tiles = n // tm
    pos_spec = lambda f: pl.BlockSpec((None, 1, tm * TOP_K), f, memory_space=pltpu.SMEM)
    return pl.pallas_call(
        _combine_kernel,
        grid_spec=pltpu.PrefetchScalarGridSpec(
            num_scalar_prefetch=1,
            grid=(n_tiles,),
            in_specs=[
                pos_spec(lambda i, tc: (i, 0, 0)),
                pos_spec(lambda i, tc: (jnp.minimum(i + 1, n_tiles - 1), 0, 0)),
                pl.BlockSpec((tm, d), lambda i, tc: (i, 0)),
                pl.BlockSpec((tm, LANES), lambda i, tc: (i, 0)),
                pl.BlockSpec((None, 1, d), lambda i, tc: (tc[i], 0, 0)),
                pl.BlockSpec(memory_space=pl.ANY),
            ],
            out_specs=pl.BlockSpec((tm, d), lambda i, tc: (i, 0)),
            scratch_shapes=[
                pltpu.VMEM((2, TOP_K, tm, d), F32),
                pltpu.SemaphoreType.DMA((2,)),
            ],
        ),
        out_shape=jax.ShapeDtypeStruct((n, d), F32),
        compiler_params=_cparams(("arbitrary",)),
        name="moe_combine",
    )(tile_cond, pos_tiles, pos_tiles, x, wts, gf, ys)


def _moe(x, h2p, meta, wts, counts, tile_cond, gf, w_gu, b_gu, w_down, b_down, layer):
    n, d = x.shape
    n_exp = w_gu.shape[1]
    rows_padded = n * TOP_K + n_exp * MOE_SUB
    pos, items = _moe_plan(meta, counts, n_exp, rows_padded)
    pos_tiles = pos.reshape(n // TOKEN_TILE, 1, TOKEN_TILE * TOP_K)
    xs = _dispatch(h2p, pos_tiles, rows_padded)
    ys = _moe_experts(xs, items, w_gu, b_gu, w_down, b_down, layer)
    return _combine(x, ys, pos_tiles, wts, tile_cond, gf)


def _tile_cond(ns, npr, seq, tm):
    return jnp.asarray(np.concatenate([1 + np.arange(ns // tm) // (seq // tm),
                                       np.zeros(npr // tm, np.int64)]).astype(np.int32))


def kernel(x_prompt, x_sample, cache_na_k, cache_na_v, state_lru, c, c_ctx, norm_mix, norm_ffn, w_ada, b_ada, ev_w_in, ev_w_out, na_q_norm, na_k_norm, na_rpb, lru_conv_w, lru_conv_b, lru_wa, lru_ba, lru_wx, lru_bx, lru_lambda, od_w_in, od_w_out, sgu_norm, sgu_w, sgu_b, router_w, router_b, moe_w_gu, moe_b_gu, moe_w_down, moe_b_down):
    bp, sp_len, d = x_prompt.shape
    bs, ss_len, _ = x_sample.shape
    depth = w_ada.shape[0]
    heads, dh = cache_na_k.shape[3], cache_na_k.shape[4]
    na_w = heads * dh
    lru_w = state_lru.shape[-1]
    n_exp = router_w.shape[-1]
    ns, npr = bs * ss_len, bp * sp_len
    assert bs == REC_SEQS and bp % REC_SEQS == 0
    assert ss_len % ROW_TILE == 0 and npr % ROW_TILE == 0 and sp_len % TOKEN_TILE == 0

    x = jnp.concatenate([x_sample.reshape(ns, d), x_prompt.reshape(npr, d)], axis=0)
    tc_proj = _tile_cond(ns, npr, ss_len, ROW_TILE)
    tc_tok = _tile_cond(ns, npr, ss_len, TOKEN_TILE)
    cond_rows = 16
    cond = jnp.concatenate([c_ctx[None, :], c, jnp.zeros((cond_rows - 1 - bs, d), F32)], axis=0)
    mod = _adaln(cond, w_ada, b_ada)
    mod = mod.reshape(depth, cond_rows, 6, d).transpose(0, 2, 1, 3).reshape(depth, 6, cond_rows, 1, d)

    new_k, new_v, new_h = [], [], []
    for layer in range(depth):
        sm, scm, gm, sf, scf, gf = [mod[layer, i] for i in range(6)]
        g_mix = norm_mix[layer][None, :]
        g_ffn = norm_ffn[layer][None, :]
        if layer % 2 == 0:
            e = layer // 2
            ev_in = ev_w_in.shape[-1]
            head_gain = jnp.concatenate([jnp.tile(na_q_norm[e], heads), jnp.tile(na_k_norm[e], heads),
                                         jnp.ones((ev_in - 2 * na_w,), F32)])[None, :]
            proj = _inproj(x, tc_proj, g_mix, scm, sm, ev_w_in[e].astype(BF16), head_gain,
                           mode="heads", n_norm_cols=2 * na_w)
            att_s = _na_attention(proj, cache_na_k[:, e].reshape(bs, -1, na_w),
                                  cache_na_v[:, e].reshape(bs, -1, na_w), _na_bias_table(na_rpb[e]),
                                  nseq=bs, seq=ss_len, heads=heads, dh=dh)
            att_p = _ctx_attention(proj, row0=ns, nseq=bp, seq=sp_len, heads=heads, dh=dh)
            wgate, bgate = _pack_gate_weights(lru_wa[e], lru_ba[e], lru_wx[e], lru_bx[e])
            rec_args = (lru_conv_w[e], lru_conv_b[e][None, :], wgate, bgate, lru_lambda[e])
            h0_s = state_lru[:, e].transpose(1, 0, 2)[None]
            rec_s, _ = _recurrent(proj, *rec_args, h0_s, row0=0, nseq=bs, seq=ss_len,
                                  xr_col=3 * na_w, yg_col=3 * na_w + lru_w, width=lru_w)
            h0_p = jnp.zeros((bp // REC_SEQS, 2, REC_SEQS, lru_w), F32)
            rec_p, hl = _recurrent(proj, *rec_args, h0_p, row0=ns, nseq=bp, seq=sp_len,
                                   xr_col=3 * na_w, yg_col=3 * na_w + lru_w, width=lru_w)
            new_k.append(proj[ns:, na_w:2 * na_w].reshape(bp, sp_len, heads, dh))
            new_v.append(proj[ns:, 2 * na_w:3 * na_w].reshape(bp, sp_len, heads, dh))
            new_h.append(hl.transpose(0, 2, 1, 3).reshape(bp, 2, lru_w))
            att = jnp.concatenate([att_s, att_p], axis=0)
            rec = jnp.concatenate([rec_s, rec_p], axis=0)
            w_out = ev_w_out[e].astype(BF16)
            lhs_list, w_list = [att, rec], [w_out[:na_w], w_out[na_w:]]
        else:
            o = layer // 2
            z = _inproj(x, tc_proj, g_mix, scm, sm, od_w_in[o].astype(BF16),
                        jnp.ones((1, od_w_in.shape[-1]), F32), mode="gelu", n_norm_cols=0)
            chunk = sgu_w.shape[-1]
            sg = _sgu(z, sgu_norm[o][None, :], sgu_w[o].astype(BF16), sgu_b[o].T, chunk=chunk)
            lhs_list, w_list = [sg], [od_w_out[o].astype(BF16)]
        rw = jnp.pad(router_w[layer], ((0, 0), (0, LANES - n_exp)))
        rw_hi, rw_lo = _split_bf16(rw)
        rb = jnp.pad(router_b[layer], (0, LANES - n_exp))[None, :]
        x, h2p, meta, wts, counts = _outproj(lhs_list, w_list, x, tc_proj, gm, g_ffn, scf, sf,
                                             rw_hi, rw_lo, rb, n_experts=n_exp)
        x = _moe(x, h2p, meta, wts, counts, tc_tok, gf, moe_w_gu, moe_b_gu, moe_w_down, moe_b_down, layer)

    y_sample = x[:ns].reshape(bs, ss_len, d)
    y_prompt = x[ns:].reshape(bp, sp_len, d)
    new_na_k = jnp.stack(new_k, axis=1)
    new_na_v = jnp.stack(new_v, axis=1)
    new_state_lru = jnp.stack(new_h, axis=1)
    return (y_prompt, y_sample, new_na_k, new_na_v, new_state_lru)
```

```python
import functools

import numpy as np
import jax
import jax.numpy as jnp
from jax import lax
from jax.experimental import pallas as pl
from jax.experimental.pallas import tpu as pltpu

F32 = jnp.float32
BF16 = jnp.bfloat16

EPS = 1e-6
NEG_INF = -1e30
GRID_W = 64
NA_KH = 8
NA_KW = 16
LRU_C = 8.0
SWIGLU_ALPHA = 1.702
SWIGLU_LIMIT = 7.0
TOP_K = 4

LANES = 128
ROW_TILE = 512
TOKEN_TILE = 256
MOE_SUB = 256
MOE_ITEM_SUBS = 7
MOE_TF = 256
MOE_TN = 512
VMEM_LIMIT = 56 * 1024 * 1024


def _cparams(sem, vmem=VMEM_LIMIT):
    return pltpu.CompilerParams(dimension_semantics=sem, vmem_limit_bytes=vmem)


def _split_bf16(x):
    hi = x.astype(BF16)
    lo = (x - hi.astype(F32)).astype(BF16)
    return hi, lo


def _gelu_tanh(x):
    return 0.5 * x * (1.0 + jnp.tanh(0.7978845608028654 * (x + 0.044715 * x * x * x)))


def _dot(a, b):
    return jnp.dot(a, b, preferred_element_type=F32)


def _dot_nt(a, b):
    return lax.dot_general(a, b, (((1,), (1,)), ((), ())), preferred_element_type=F32)


def _adaln_kernel(c_ref, w_ref, b_ref, o_ref):
    cnd = c_ref[...]
    s = cnd * jax.nn.sigmoid(cnd)
    hi, lo = _split_bf16(s)
    w = w_ref[...].astype(BF16)
    o_ref[...] = _dot(hi, w) + _dot(lo, w) + b_ref[...]


def _adaln(cond, w_ada, b_ada, tn=1024):
    depth, d, n6 = w_ada.shape
    rows = cond.shape[0]
    return pl.pallas_call(
        _adaln_kernel,
        grid=(depth, n6 // tn),
        in_specs=[
            pl.BlockSpec((rows, d), lambda l, j: (0, 0)),
            pl.BlockSpec((None, d, tn), lambda l, j: (l, 0, j)),
            pl.BlockSpec((None, 1, tn), lambda l, j: (l, 0, j)),
        ],
        out_specs=pl.BlockSpec((None, rows, tn), lambda l, j: (l, 0, j)),
        out_shape=jax.ShapeDtypeStruct((depth, rows, n6), F32),
        compiler_params=_cparams(("arbitrary", "arbitrary")),
        name="adaln",
    )(cond, w_ada, b_ada.reshape(depth, 1, n6))


def _modulated_norm(x, g, sc, sh):
    ms = jnp.mean(x * x, axis=-1, keepdims=True)
    return (x * lax.rsqrt(ms + EPS) * g) * (1.0 + sc) + sh


def _inproj_kernel(tc_ref, x_ref, g_ref, sc_ref, sh_ref, w_ref, hg_ref, o_ref, h_scr, *,
                   mode, n_norm_tiles):
    j = pl.program_id(1)

    @pl.when(j == 0)
    def _():
        h_scr[...] = _modulated_norm(x_ref[...], g_ref[...], sc_ref[...], sh_ref[...]).astype(BF16)

    acc = _dot(h_scr[...], w_ref[...])
    if mode == "gelu":
        o_ref[...] = _gelu_tanh(acc)
    else:
        @pl.when(j < n_norm_tiles)
        def _():
            hg = hg_ref[...]
            for c in range(acc.shape[1] // LANES):
                blk = acc[:, c * LANES:(c + 1) * LANES]
                ms = jnp.mean(blk * blk, axis=-1, keepdims=True)
                o_ref[:, c * LANES:(c + 1) * LANES] = (
                    blk * lax.rsqrt(ms + EPS) * hg[:, c * LANES:(c + 1) * LANES])

        @pl.when(j >= n_norm_tiles)
        def _():
            o_ref[...] = acc


def _inproj(x, tile_cond, norm_g, sc, sh, w_bf16, head_gain, *, mode, n_norm_cols, tn=1024):
    n, d = x.shape
    nout = w_bf16.shape[1]
    tm = ROW_TILE
    kern = functools.partial(_inproj_kernel, mode=mode, n_norm_tiles=n_norm_cols // tn)
    modspec = pl.BlockSpec((None, 1, d), lambda i, j, tc: (tc[i], 0, 0))
    return pl.pallas_call(
        kern,
        grid_spec=pltpu.PrefetchScalarGridSpec(
            num_scalar_prefetch=1,
            grid=(n // tm, nout // tn),
            in_specs=[
                pl.BlockSpec((tm, d), lambda i, j, tc: (i, 0)),
                pl.BlockSpec((1, d), lambda i, j, tc: (0, 0)),
                modspec, modspec,
                pl.BlockSpec((d, tn), lambda i, j, tc: (0, j)),
                pl.BlockSpec((1, tn), lambda i, j, tc: (0, j)),
            ],
            out_specs=pl.BlockSpec((tm, tn), lambda i, j, tc: (i, j)),
            scratch_shapes=[pltpu.VMEM((tm, d), BF16)],
        ),
        out_shape=jax.ShapeDtypeStruct((n, nout), F32),
        compiler_params=_cparams(("arbitrary", "arbitrary")),
        name="inproj_" + mode,
    )(tile_cond, x, norm_g, sc, sh, w_bf16, head_gain)


def _ctx_attn_kernel(q_ref, k_ref, v_ref, o_ref, *, heads, dh):
    scale = dh ** -0.5
    for h in range(heads):
        sl = slice(h * dh, (h + 1) * dh)
        q = q_ref[:, sl].astype(BF16)
        k = k_ref[:, sl].astype(BF16)
        v = v_ref[:, sl].astype(BF16)
        s = _dot_nt(q, k) * scale
        m = jnp.max(s, axis=-1, keepdims=True)
        p = jnp.exp(s - m)
        den = jnp.sum(p, axis=-1, keepdims=True)
        o = _dot(p.astype(BF16), v) / den
        o_ref[:, sl] = o.astype(o_ref.dtype)


def _ctx_attention(proj, *, row0, nseq, seq, heads, dh):
    width = heads * dh
    rb0 = row0 // seq
    kern = functools.partial(_ctx_attn_kernel, heads=heads, dh=dh)
    return pl.pallas_call(
        kern,
        grid=(nseq,),
        in_specs=[
            pl.BlockSpec((seq, width), lambda b: (rb0 + b, 0)),
            pl.BlockSpec((seq, width), lambda b: (rb0 + b, 1)),
            pl.BlockSpec((seq, width), lambda b: (rb0 + b, 2)),
        ],
        out_specs=pl.BlockSpec((seq, width), lambda b: (b, 0)),
        out_shape=jax.ShapeDtypeStruct((nseq * seq, width), BF16),
        compiler_params=_cparams(("arbitrary",)),
        name="ctx_attention",
    )(proj, proj, proj)


def _na_attn_kernel(q_ref, k_ref, v_ref, kc_ref, vc_ref, bias_ref, o_ref, *, rows, dh):
    scale = dh ** -0.5
    win = NA_KH * GRID_W
    qc = lax.broadcasted_iota(jnp.int32, (GRID_W, win), 0)
    kc = lax.broadcasted_iota(jnp.int32, (GRID_W, win), 1) % GRID_W
    c_start = jnp.clip(qc - NA_KW // 2, 0, GRID_W - NA_KW)
    col_ok = (kc >= c_start) & (kc < c_start + NA_KW)
    k_ctx = kc_ref[...].astype(BF16)
    v_ctx = vc_ref[...].astype(BF16)
    for r in range(rows):
        rs = min(max(r - NA_KH // 2, 0), rows - NA_KH)
        q = q_ref[r * GRID_W:(r + 1) * GRID_W, :].astype(BF16)
        kw = k_ref[rs * GRID_W:rs * GRID_W + win, :].astype(BF16)
        vw = v_ref[rs * GRID_W:rs * GRID_W + win, :].astype(BF16)
        s_win = _dot_nt(q, kw) * scale + bias_ref[r - rs]
        s_win = jnp.where(col_ok, s_win, NEG_INF)
        s_ctx = _dot_nt(q, k_ctx) * scale
        m = jnp.maximum(jnp.max(s_win, axis=-1, keepdims=True),
                        jnp.max(s_ctx, axis=-1, keepdims=True))
        p_win = jnp.exp(s_win - m)
        p_ctx = jnp.exp(s_ctx - m)
        den = jnp.sum(p_win, axis=-1, keepdims=True) + jnp.sum(p_ctx, axis=-1, keepdims=True)
        o = (_dot(p_win.astype(BF16), vw) + _dot(p_ctx.astype(BF16), v_ctx)) / den
        o_ref[r * GRID_W:(r + 1) * GRID_W, :] = o.astype(o_ref.dtype)


def _na_bias_kernel(rpb_ref, o_ref, *, n_rows, n_cols):
    h = pl.program_id(0)
    q = lax.broadcasted_iota(jnp.int32, (GRID_W, GRID_W), 0)
    k = lax.broadcasted_iota(jnp.int32, (GRID_W, GRID_W), 1)
    col_idx = jnp.clip(k - q + NA_KW - 1, 0, n_cols - 1)
    for ri in range(n_rows):
        acc = jnp.zeros((GRID_W, GRID_W), F32)
        for j in range(n_cols):
            acc = jnp.where(col_idx == j, rpb_ref[(h * n_rows + ri) * n_cols + j], acc)
        o_ref[ri] = acc


def _na_bias_table(rpb):
    heads, n_rows, n_cols = rpb.shape
    toep = pl.pallas_call(
        functools.partial(_na_bias_kernel, n_rows=n_rows, n_cols=n_cols),
        grid=(heads,),
        in_specs=[pl.BlockSpec(memory_space=pltpu.SMEM)],
        out_specs=pl.BlockSpec((None, n_rows, GRID_W, GRID_W), lambda h: (h, 0, 0, 0)),
        out_shape=jax.ShapeDtypeStruct((heads, n_rows, GRID_W, GRID_W), F32),
        compiler_params=_cparams(("arbitrary",)),
        name="na_bias",
    )(rpb.reshape(-1))
    classes = []
    for d in range(NA_KH):
        t = toep[:, NA_KH - 1 - d:2 * NA_KH - 1 - d]
        classes.append(t.transpose(0, 2, 1, 3).reshape(heads, GRID_W, NA_KH * GRID_W))
    return jnp.stack(classes, axis=1)


def _na_attention(proj, cache_k, cache_v, bias_tab, *, nseq, seq, heads, dh):
    past = cache_k.shape[1]
    rows = seq // GRID_W
    kern = functools.partial(_na_attn_kernel, rows=rows, dh=dh)
    return pl.pallas_call(
        kern,
        grid=(heads, nseq),
        in_specs=[
            pl.BlockSpec((seq, dh), lambda h, b: (b, h)),
            pl.BlockSpec((seq, dh), lambda h, b: (b, heads + h)),
            pl.BlockSpec((seq, dh), lambda h, b: (b, 2 * heads + h)),
            pl.BlockSpec((None, past, dh), lambda h, b: (b, 0, h)),
            pl.BlockSpec((None, past, dh), lambda h, b: (b, 0, h)),
            pl.BlockSpec((None, NA_KH, GRID_W, NA_KH * GRID_W), lambda h, b: (h, 0, 0, 0)),
        ],
        out_specs=pl.BlockSpec((seq, dh), lambda h, b: (b, h)),
        out_shape=jax.ShapeDtypeStruct((nseq * seq, heads * dh), BF16),
        compiler_params=_cparams(("arbitrary", "arbitrary")),
        name="na_attention",
    )(proj, proj, proj, cache_k, cache_v, bias_tab)


REC_SEQS = 8
REC_CHUNK = 256


def _rec_kernel(xr_ref, yg_ref, cw_ref, cb_ref, wg_ref, bg_ref, lam_ref, h0_ref,
                out_ref, hl_ref, a_scr, b_scr, *, seq):
    pitch = seq + 8
    total = REC_SEQS * seq
    nchunks = total // REC_CHUNK
    row = lax.broadcasted_iota(jnp.int32, (REC_CHUNK, LANES), 0)
    cw = cw_ref[...]
    cb = cb_ref[...]
    wg_hi = wg_ref[0]
    wg_lo = wg_ref[1]
    bg = bg_ref[...]
    lam = lam_ref[...]
    sp = jnp.maximum(-lam, 0.0) + jnp.log(1.0 + jnp.exp(-jnp.abs(lam)))

    def coeff_chunk(ci, carry):
        r0 = pl.multiple_of(ci * REC_CHUNK, REC_CHUNK)
        t0 = r0 % seq
        x = xr_ref[pl.ds(r0, REC_CHUNK), :]
        prev = xr_ref[pl.ds(pl.multiple_of(jnp.maximum(r0 - 8, 0), 8), 8), :]
        nxt = xr_ref[pl.ds(pl.multiple_of(jnp.minimum(r0 + REC_CHUNK, total - 8), 8), 8), :]
        has_prev = (t0 > 0).astype(F32)
        has_next = (t0 + REC_CHUNK < seq).astype(F32)
        p6 = prev[6:7, :] * has_prev
        p7 = prev[7:8, :] * has_prev
        n0 = nxt[0:1, :] * has_next
        x_m1 = jnp.where(row == 0, p7, pltpu.roll(x, 1, axis=0))
        x_m2 = jnp.where(row == 0, p6, jnp.where(row == 1, p7, pltpu.roll(x, 2, axis=0)))
        x_p1 = jnp.where(row == REC_CHUNK - 1, n0, pltpu.roll(x, REC_CHUNK - 1, axis=0))
        xc = cb + x_m2 * cw[0:1, :] + x_m1 * cw[1:2, :] + x * cw[2:3, :] + x_p1 * cw[3:4, :]
        xh, xl = _split_bf16(xc)
        gates = _dot(xh, wg_hi) + _dot(xl, wg_hi) + _dot(xh, wg_lo) + bg
        srow = pl.multiple_of((r0 // seq) * pitch + t0, 8)
        for d in range(2):
            r = jax.nn.sigmoid(gates[:, (2 * d) * LANES:(2 * d + 1) * LANES])
            i = jax.nn.sigmoid(gates[:, (2 * d + 1) * LANES:(2 * d + 2) * LANES])
            log_a = (-LRU_C) * r * sp[d:d + 1, :]
            a = jnp.exp(log_a)
            b = jnp.sqrt(1.0 - jnp.exp(2.0 * log_a)) * (i * xc)
            a_scr[d, pl.ds(srow, REC_CHUNK), :] = a
            b_scr[d, pl.ds(srow, REC_CHUNK), :] = b
        return carry

    lax.fori_loop(0, nchunks, coeff_chunk, 0)

    def scan_step(t, carry):
        hf, hb = carry
        tb = seq - 1 - t
        hf = a_scr[0, pl.ds(t, REC_SEQS, stride=pitch), :] * hf + b_scr[0, pl.ds(t, REC_SEQS, stride=pitch), :]
        b_scr[0, pl.ds(t, REC_SEQS, stride=pitch), :] = hf
        hb = a_scr[1, pl.ds(tb, REC_SEQS, stride=pitch), :] * hb + b_scr[1, pl.ds(tb, REC_SEQS, stride=pitch), :]
        b_scr[1, pl.ds(tb, REC_SEQS, stride=pitch), :] = hb
        return hf, hb

    hf, hb = lax.fori_loop(0, seq, scan_step, (h0_ref[0], h0_ref[1]), unroll=8)
    hl_ref[0] = hf
    hl_ref[1] = hb

    def out_chunk(ci, carry):
        r0 = pl.multiple_of(ci * REC_CHUNK, REC_CHUNK)
        srow = pl.multiple_of((r0 // seq) * pitch + r0 % seq, 8)
        hsum = b_scr[0, pl.ds(srow, REC_CHUNK), :] + b_scr[1, pl.ds(srow, REC_CHUNK), :]
        out_ref[pl.ds(r0, REC_CHUNK), :] = (_gelu_tanh(yg_ref[pl.ds(r0, REC_CHUNK), :]) * hsum).astype(out_ref.dtype)
        return carry

    lax.fori_loop(0, nchunks, out_chunk, 0)


def _recurrent(proj, conv_w, conv_b, wgate, bgate, lam, h0, *, row0, nseq, seq, xr_col, yg_col, width):
    groups = nseq // REC_SEQS
    blk_rows = REC_SEQS * seq
    rb0 = row0 // blk_rows
    nblk = width // LANES
    xc0 = xr_col // LANES
    yc0 = yg_col // LANES
    pitch = seq + 8
    kern = functools.partial(_rec_kernel, seq=seq)
    return pl.pallas_call(
        kern,
        grid=(groups, nblk),
        in_specs=[
            pl.BlockSpec((blk_rows, LANES), lambda g, n: (rb0 + g, xc0 + n)),
            pl.BlockSpec((blk_rows, LANES), lambda g, n: (rb0 + g, yc0 + n)),
            pl.BlockSpec((4, LANES), lambda g, n: (0, n)),
            pl.BlockSpec((1, LANES), lambda g, n: (0, n)),
            pl.BlockSpec((None, 2, LANES, 4 * LANES), lambda g, n: (n, 0, 0, 0)),
            pl.BlockSpec((None, 1, 4 * LANES), lambda g, n: (n, 0, 0)),
            pl.BlockSpec((2, LANES), lambda g, n: (0, n)),
            pl.BlockSpec((None, 2, REC_SEQS, LANES), lambda g, n: (g, 0, 0, n)),
        ],
        out_specs=[
            pl.BlockSpec((blk_rows, LANES), lambda g, n: (g, n)),
            pl.BlockSpec((None, 2, REC_SEQS, LANES), lambda g, n: (g, 0, 0, n)),
        ],
        out_shape=[
            jax.ShapeDtypeStruct((nseq * seq, width), BF16),
            jax.ShapeDtypeStruct((groups, 2, REC_SEQS, width), F32),
        ],
        scratch_shapes=[
            pltpu.VMEM((2, REC_SEQS * pitch, LANES), F32),
            pltpu.VMEM((2, REC_SEQS * pitch, LANES), F32),
        ],
        compiler_params=_cparams(("arbitrary", "arbitrary")),
        name="recurrent",
    )(proj, proj, conv_w, conv_b, wgate, bgate, lam, h0)


def _pack_gate_weights(wa, ba, wx, bx):
    nblk = wa.shape[1]
    w = jnp.concatenate([wa[0], wx[0], wa[1], wx[1]], axis=-1)
    hi, lo = _split_bf16(w)
    b = jnp.stack([ba[0].reshape(nblk, LANES), bx[0].reshape(nblk, LANES),
                   ba[1].reshape(nblk, LANES), bx[1].reshape(nblk, LANES)], axis=1)
    return jnp.stack([hi, lo], axis=1), b.reshape(nblk, 1, 4 * LANES)


def _sgu_kernel(u_ref, v_ref, g_ref, ws_ref, bs_ref, o_ref, *, chunk, groups):
    g = g_ref[...]
    for c in range(u_ref.shape[0] // chunk):
        rs = slice(c * chunk, (c + 1) * chunk)
        v = v_ref[rs, :]
        ms = jnp.mean(v * v, axis=-1, keepdims=True)
        vn = (v * lax.rsqrt(ms + EPS) * g).astype(BF16)
        for grp in range(groups):
            cs = slice(grp * LANES, (grp + 1) * LANES)
            mixed = _dot(ws_ref[grp], vn[:, cs]) + bs_ref[:, grp:grp + 1]
            o_ref[rs, cs] = (u_ref[rs, cs] * mixed).astype(o_ref.dtype)


def _sgu(z, v_norm, ws_bf16, bs_t, *, chunk, tm=512):
    n, w2 = z.shape
    w = w2 // 2
    groups = ws_bf16.shape[0]
    kern = functools.partial(_sgu_kernel, chunk=chunk, groups=groups)
    return pl.pallas_call(
        kern,
        grid=(n // tm,),
        in_specs=[
            pl.BlockSpec((tm, w), lambda i: (i, 0)),
            pl.BlockSpec((tm, w), lambda i: (i, 1)),
            pl.BlockSpec((1, w), lambda i: (0, 0)),
            pl.BlockSpec((groups, chunk, chunk), lambda i: (0, 0, 0)),
            pl.BlockSpec((chunk, groups), lambda i: (0, 0)),
        ],
        out_specs=pl.BlockSpec((tm, w), lambda i: (i, 0)),
        out_shape=jax.ShapeDtypeStruct((n, w), BF16),
        compiler_params=_cparams(("arbitrary",)),
        name="sgu",
    )(z, z, v_norm, ws_bf16, bs_t)


def _outproj_kernel(tc_ref, *refs, n_lhs, n_experts):
    lhs = refs[:n_lhs]
    ws = refs[n_lhs:2 * n_lhs]
    (x_ref, gm_ref, g2_ref, scf_ref, sf_ref, rwh_ref, rwl_ref, rb_ref, tri_ref,
     xo_ref, h2_ref, meta_ref, wts_ref, cnt_ref, carry) = refs[2 * n_lhs:]
    @pl.when(pl.program_id(0) == 0)
    def _():
        carry[...] = jnp.zeros(carry.shape, carry.dtype)

    mix = _dot(lhs[0][...], ws[0][...])
    for a_ref, w_ref in zip(lhs[1:], ws[1:]):
        mix = mix + _dot(a_ref[...], w_ref[...])
    xn = x_ref[...] + gm_ref[...] * mix
    xo_ref[...] = xn
    h2 = _modulated_norm(xn, g2_ref[...], scf_ref[...], sf_ref[...])
    hi, lo = _split_bf16(h2)
    h2_ref[...] = h2
    logits = _dot(hi, rwh_ref[...]) + _dot(lo, rwh_ref[...]) + _dot(hi, rwl_ref[...]) + rb_ref[...]
    lane = lax.broadcasted_iota(jnp.int32, logits.shape, 1)
    l = jnp.where(lane < n_experts, logits, -jnp.inf)
    vals, idxs = [], []
    for _ in range(TOP_K):
        m = jnp.max(l, axis=-1, keepdims=True)
        i = jnp.min(jnp.where(l == m, lane, LANES), axis=-1, keepdims=True)
        vals.append(m)
        idxs.append(i)
        l = jnp.where(lane == i, -jnp.inf, l)
    es = [jnp.exp(v - vals[0]) for v in vals]
    den = es[0] + es[1] + es[2] + es[3]
    chosen = jnp.zeros(logits.shape, F32)
    for k in range(TOP_K):
        chosen = chosen + (lane == idxs[k]).astype(F32)
    before = carry[...] + _dot(tri_ref[...], chosen.astype(BF16))
    meta = jnp.zeros(logits.shape, jnp.int32)
    wts_out = jnp.zeros(logits.shape, F32)
    for k in range(TOP_K):
        rank = jnp.sum(jnp.where(lane == idxs[k], before, 0.0), axis=-1, keepdims=True).astype(jnp.int32)
        meta = jnp.where(lane == k, idxs[k], meta)
        meta = jnp.where(lane == TOP_K + k, rank, meta)
        wts_out = jnp.where(lane == k, es[k] / den, wts_out)
    meta_ref[...] = meta
    wts_ref[...] = wts_out
    total = carry[...] + jnp.sum(chosen, axis=0, keepdims=True)
    carry[...] = total
    cnt_ref[...] = total


def _outproj(lhs_list, w_list, x, tile_cond, gm, g2, scf, sf, rw_hi, rw_lo, rb, *, n_experts):
    n, d = x.shape
    tm = ROW_TILE
    n_lhs = len(lhs_list)
    kern = functools.partial(_outproj_kernel, n_lhs=n_lhs, n_experts=n_experts)
    modspec = pl.BlockSpec((None, 1, d), lambda i, tc: (tc[i], 0, 0))
    tri = jnp.asarray(np.tril(np.ones((tm, tm), np.float32), -1), BF16)
    in_specs = (
        [pl.BlockSpec((tm, a.shape[1]), lambda i, tc: (i, 0)) for a in lhs_list]
        + [pl.BlockSpec(w.shape, lambda i, tc: (0, 0)) for w in w_list]
        + [pl.BlockSpec((tm, d), lambda i, tc: (i, 0)),
           modspec,
           pl.BlockSpec((1, d), lambda i, tc: (0, 0)),
           modspec, modspec,
           pl.BlockSpec((d, LANES), lambda i, tc: (0, 0)),
           pl.BlockSpec((d, LANES), lambda i, tc: (0, 0)),
           pl.BlockSpec((1, LANES), lambda i, tc: (0, 0)),
           pl.BlockSpec((tm, tm), lambda i, tc: (0, 0))])
    return pl.pallas_call(
        kern,
        grid_spec=pltpu.PrefetchScalarGridSpec(
            num_scalar_prefetch=1,
            grid=(n // tm,),
            in_specs=in_specs,
            out_specs=[
                pl.BlockSpec((tm, d), lambda i, tc: (i, 0)),
                pl.BlockSpec((tm, d), lambda i, tc: (i, 0)),
                pl.BlockSpec((tm, LANES), lambda i, tc: (i, 0)),
                pl.BlockSpec((tm, LANES), lambda i, tc: (i, 0)),
                pl.BlockSpec((1, LANES), lambda i, tc: (0, 0)),
            ],
            scratch_shapes=[pltpu.VMEM((1, LANES), F32)],
        ),
        out_shape=[
            jax.ShapeDtypeStruct((n, d), F32),
            jax.ShapeDtypeStruct((n, d), F32),
            jax.ShapeDtypeStruct((n, LANES), jnp.int32),
            jax.ShapeDtypeStruct((n, LANES), F32),
            jax.ShapeDtypeStruct((1, LANES), F32),
        ],
        compiler_params=_cparams(("arbitrary",)),
        name="outproj_router",
    )(tile_cond, *lhs_list, *w_list, x, gm, g2, scf, sf, rw_hi, rw_lo, rb, tri)


def _dispatch_kernel(pos_ref, h_ref, xs_in, xs_hbm, sem):
    del xs_in
    rows = h_ref.shape[0]

    def issue(r, carry):
        for k in range(TOP_K):
            p = pos_ref[0, r * TOP_K + k]
            pltpu.make_async_copy(h_ref.at[pl.ds(r, 1), :], xs_hbm.at[pl.ds(p, 1), :], sem).start()
        return carry

    lax.fori_loop(0, rows, issue, 0, unroll=4)
    for k in range(TOP_K):
        pltpu.make_async_copy(h_ref, xs_hbm.at[pl.ds(0, rows), :], sem).wait()


def _dispatch(h2, pos_tiles, rows_padded):
    n, d = h2.shape
    tm = TOKEN_TILE
    xs0 = jnp.zeros((rows_padded, d), h2.dtype)
    return pl.pallas_call(
        _dispatch_kernel,
        grid=(n // tm,),
        in_specs=[
            pl.BlockSpec((None, 1, tm * TOP_K), lambda i: (i, 0, 0), memory_space=pltpu.SMEM),
            pl.BlockSpec((tm, d), lambda i: (i, 0)),
            pl.BlockSpec(memory_space=pl.ANY),
        ],
        out_specs=pl.BlockSpec(memory_space=pl.ANY),
        out_shape=jax.ShapeDtypeStruct((rows_padded, d), h2.dtype),
        scratch_shapes=[pltpu.SemaphoreType.DMA],
        input_output_aliases={2: 0},
        compiler_params=_cparams(("arbitrary",)),
        name="moe_dispatch",
    )(pos_tiles, h2, xs0)


def _moe_kernel(ie_ref, it0_ref, int_ref, inz_ref, xs_hbm, wgu_hbm, wd_hbm, bgu_ref, bd_ref,
                ys_hbm, stage, xb, act, out, wf, wb, sem_x, sem_w, sem_out, *, layer, n1, n2, f):
    s = pl.program_id(0)
    n_items = pl.num_programs(0)
    nch = n1 + n2
    nt = int_ref[s]
    nz = inz_ref[s]
    t0 = it0_ref[s]
    e = ie_ref[s]
    sn = jnp.minimum(s + 1, n_items - 1)
    has_next = (s + 1 < n_items) & (int_ref[sn] > 0)
    e_next = ie_ref[sn]
    t0_next = it0_ref[sn]
    nt_next = int_ref[sn]
    cast_rows = wf.shape[1] // ((MOE_ITEM_SUBS + 1) // 2)

    def w_start(ex, kk, slot):
        @pl.when(kk < n1)
        def _():
            col = pl.multiple_of(kk * MOE_TF, MOE_TF)
            pltpu.make_async_copy(wgu_hbm.at[layer, ex, :, pl.ds(col, MOE_TF)],
                                  wf.at[slot, :, pl.ds(0, MOE_TF)], sem_w.at[slot]).start()
            pltpu.make_async_copy(wgu_hbm.at[layer, ex, :, pl.ds(f + col, MOE_TF)],
                                  wf.at[slot, :, pl.ds(MOE_TF, MOE_TF)], sem_w.at[slot]).start()

        @pl.when(kk >= n1)
        def _():
            col = pl.multiple_of((kk - n1) * MOE_TN, MOE_TN)
            pltpu.make_async_copy(wd_hbm.at[layer, ex, :, pl.ds(col, MOE_TN)], wf.at[slot], sem_w.at[slot]).start()

    def w_wait(slot):
        pltpu.make_async_copy(wd_hbm.at[layer, 0, :, pl.ds(0, MOE_TN)], wf.at[slot], sem_w.at[slot]).wait()

    def cast_slice(slot, q):
        r0 = pl.multiple_of(q * cast_rows, cast_rows)
        wb[slot, pl.ds(r0, cast_rows), :] = wf[slot, pl.ds(r0, cast_rows), :].astype(BF16)

    def x_copy(tile, slot):
        return pltpu.make_async_copy(xs_hbm.at[pl.ds(tile * MOE_SUB, MOE_SUB), :], stage.at[slot], sem_x.at[slot])

    def x_land(slot, j):
        r0 = pl.multiple_of(j * MOE_SUB, MOE_SUB)
        xb[pl.ds(r0, MOE_SUB), :] = stage[slot].astype(BF16)

    def copy_out(n, j):
        col = n * MOE_TN if isinstance(n, int) else pl.multiple_of(n * MOE_TN, MOE_TN)
        return pltpu.make_async_copy(
            out.at[n, pl.ds(j * MOE_SUB, MOE_SUB), :],
            ys_hbm.at[pl.ds((t0 + j) * MOE_SUB, MOE_SUB), pl.ds(col, MOE_TN)],
            sem_out.at[n])

    def sub_tiles(fn, other):
        def pair(p, carry):
            fn(2 * p)
            fn(2 * p + 1)
            cast_slice(other, p)
            return carry

        lax.fori_loop(0, nt // 2, pair, 0)

        @pl.when(nt % 2 == 1)
        def _():
            fn(nt - 1)
            cast_slice(other, nt // 2)

        trips = nt // 2 + nt % 2
        for q in range((MOE_ITEM_SUBS + 1) // 2):
            @pl.when(q >= trips)
            def _():
                cast_slice(other, q)

    def advance(k):
        slot = k % 2
        k2 = k + 2

        @pl.when(k2 < nch)
        def _():
            w_start(e, k2, slot)

        @pl.when((k2 >= nch) & has_next)
        def _():
            w_start(e_next, k2 - nch, slot)

        @pl.when((k + 1 < nch) | has_next)
        def _():
            w_wait(1 - slot)
        return slot

    @pl.when(nz > 0)
    def _():
        out[0, pl.ds(0, MOE_SUB), :] = jnp.zeros((MOE_SUB, MOE_TN), out.dtype)

        def copy_zero(n, j):
            return pltpu.make_async_copy(
                out.at[0, pl.ds(0, MOE_SUB), :],
                ys_hbm.at[pl.ds((t0 + j) * MOE_SUB, MOE_SUB), pl.ds(n * MOE_TN, MOE_TN)], sem_out.at[n])

        for j in range(MOE_ITEM_SUBS):
            @pl.when(j < nz)
            def _():
                for n in range(n2):
                    copy_zero(n, j).start()
        for j in range(MOE_ITEM_SUBS):
            @pl.when(j < nz)
            def _():
                for n in range(n2):
                    copy_zero(n, j).wait()

    @pl.when((s == 0) & (nt > 0))
    def _():
        w_start(e, 0, 0)
        w_start(e, 1, 1)
        x_copy(t0, 0).start()
        for j in range(MOE_ITEM_SUBS):
            @pl.when(j < nt)
            def _():
                if j + 1 < MOE_ITEM_SUBS:
                    @pl.when(j + 1 < nt)
                    def _():
                        x_copy(t0 + j + 1, (j + 1) % 2).start()
                x_copy(t0 + j, j % 2).wait()
                x_land(j % 2, j)
        w_wait(0)
        for q in range((MOE_ITEM_SUBS + 1) // 2):
            cast_slice(0, q)

    @pl.when(nt > 0)
    def _():
        def up_step(k, carry):
            slot = advance(k)
            bg = bgu_ref[k]
            bl = bgu_ref[n1 + k]

            def up(j):
                r0 = pl.multiple_of(j * MOE_SUB, MOE_SUB)
                x = xb[pl.ds(r0, MOE_SUB), :]
                glu = jnp.minimum(_dot(x, wb[slot, :, pl.ds(0, MOE_TF)]) + bg, SWIGLU_LIMIT)
                lin = jnp.clip(_dot(x, wb[slot, :, pl.ds(MOE_TF, MOE_TF)]) + bl, -SWIGLU_LIMIT, SWIGLU_LIMIT)
                act[k, pl.ds(r0, MOE_SUB), :] = (glu * jax.nn.sigmoid(SWIGLU_ALPHA * glu) * (lin + 1.0)).astype(BF16)

            sub_tiles(up, 1 - slot)
            return carry

        lax.fori_loop(0, n1, up_step, 0)

        def down_step(n, carry):
            slot = advance(n1 + n)
            bd = bd_ref[n]
            for jj in range(2):
                @pl.when(has_next & (2 * n + jj < nt_next))
                def _():
                    x_copy(t0_next + 2 * n + jj, jj).start()

            def down(j):
                r0 = pl.multiple_of(j * MOE_SUB, MOE_SUB)
                a = jnp.concatenate([act[k, pl.ds(r0, MOE_SUB), :] for k in range(n1)], axis=1)
                out[n, pl.ds(r0, MOE_SUB), :] = _dot(a, wb[slot]) + bd

            sub_tiles(down, 1 - slot)
            for j in range(MOE_ITEM_SUBS):
                @pl.when(j < nt)
                def _():
                    copy_out(n, j).start()
            for jj in range(2):
                @pl.when(has_next & (2 * n + jj < nt_next))
                def _():
                    x_copy(t0_next + 2 * n + jj, jj).wait()
                    x_land(jj, 2 * n + jj)
            return carry

        lax.fori_loop(0, n2, down_step, 0)
        for n in range(n2):
            for j in range(MOE_ITEM_SUBS):
                @pl.when(j < nt)
                def _():
                    copy_out(n, j).wait()


def _moe_experts(xs, items, w_gu, b_gu, w_down, b_down, layer):
    rp, d = xs.shape
    n_layers, n_exp, _, f2 = w_gu.shape
    f = f2 // 2
    assert f == d and MOE_TN == 2 * MOE_TF and MOE_ITEM_SUBS <= 2 * (d // MOE_TN)
    n1 = f // MOE_TF
    n2 = d // MOE_TN
    item_e, item_t0, item_nt, item_nz = items
    n_items = item_e.shape[0]
    tm = MOE_ITEM_SUBS * MOE_SUB
    kern = functools.partial(_moe_kernel, layer=layer, n1=n1, n2=n2, f=f)
    return pl.pallas_call(
        kern,
        grid_spec=pltpu.PrefetchScalarGridSpec(
            num_scalar_prefetch=4,
            grid=(n_items,),
            in_specs=[
                pl.BlockSpec(memory_space=pl.ANY),
                pl.BlockSpec(memory_space=pl.ANY),
                pl.BlockSpec(memory_space=pl.ANY),
                pl.BlockSpec((None, None, 2 * n1, 1, MOE_TF), lambda s, ie, it0, nt, nz: (layer, ie[s], 0, 0, 0)),
                pl.BlockSpec((None, None, n2, 1, MOE_TN), lambda s, ie, it0, nt, nz: (layer, ie[s], 0, 0, 0)),
            ],
            out_specs=pl.BlockSpec(memory_space=pl.ANY),
            scratch_shapes=[
                pltpu.VMEM((2, MOE_SUB, d), F32),
                pltpu.VMEM((tm, d), BF16),
                pltpu.VMEM((n1, tm, MOE_TF), BF16),
                pltpu.VMEM((n2, tm, MOE_TN), F32),
                pltpu.VMEM((2, d, MOE_TN), F32),
                pltpu.VMEM((2, d, MOE_TN), BF16),
                pltpu.SemaphoreType.DMA((2,)),
                pltpu.SemaphoreType.DMA((2,)),
                pltpu.SemaphoreType.DMA((n2,)),
            ],
        ),
        out_shape=jax.ShapeDtypeStruct((rp, d), F32),
        compiler_params=_cparams(("arbitrary",)),
        name="moe_experts",
    )(item_e, item_t0, item_nt, item_nz, xs, w_gu, w_down,
      b_gu.reshape(n_layers, n_exp, 2 * n1, 1, MOE_TF), b_down.reshape(n_layers, n_exp, n2, 1, MOE_TN))


def _moe_plan(meta, counts, n_experts, rows_padded):
    n = meta.shape[0]
    top_i = meta[:, :TOP_K]
    rank = meta[:, TOP_K:2 * TOP_K]
    counts = counts[0, :n_experts].astype(jnp.int32)
    ntiles = (counts + MOE_SUB - 1) // MOE_SUB
    tile_end = jnp.cumsum(ntiles)
    tile_start = tile_end - ntiles
    experts = jnp.arange(n_experts, dtype=jnp.int32)
    start_sel = jnp.sum(jnp.where(top_i[:, :, None] == experts[None, None, :], tile_start[None, None, :], 0), axis=-1)
    pos = start_sel * MOE_SUB + rank
    n_items_max = n_experts + (n * TOP_K // MOE_SUB + MOE_ITEM_SUBS - 1) // MOE_ITEM_SUBS
    items_e = (ntiles + MOE_ITEM_SUBS - 1) // MOE_ITEM_SUBS
    item_end = jnp.cumsum(items_e)
    item_start = item_end - items_e
    sidx = jnp.arange(n_items_max, dtype=jnp.int32)
    valid = sidx < item_end[-1]
    ie = jnp.sum((sidx[:, None] >= item_end[None, :]).astype(jnp.int32), axis=1)
    ie = jnp.minimum(ie, n_experts - 1)
    pick = ie[:, None] == experts[None, :]
    local = sidx - jnp.sum(jnp.where(pick, item_start[None, :], 0), axis=1)
    t0 = jnp.sum(jnp.where(pick, tile_start[None, :], 0), axis=1) + local * MOE_ITEM_SUBS
    nt = jnp.clip(jnp.sum(jnp.where(pick, ntiles[None, :], 0), axis=1) - local * MOE_ITEM_SUBS, 0, MOE_ITEM_SUBS)
    last_e = jnp.max(jnp.where(valid, ie, 0))
    ie = jnp.where(valid, ie, last_e).astype(jnp.int32)
    z0 = tile_end[-1] + (sidx - item_end[-1]) * MOE_ITEM_SUBS
    nz = jnp.clip(rows_padded // MOE_SUB - z0, 0, MOE_ITEM_SUBS)
    t0 = jnp.where(valid, t0, z0).astype(jnp.int32)
    nt = jnp.where(valid, nt, 0).astype(jnp.int32)
    nz = jnp.where(valid, 0, nz).astype(jnp.int32)
    return pos.astype(jnp.int32), (ie, t0, nt, nz)


def _combine_kernel(tc_ref, pos_ref, posn_ref, x_ref, w_ref, gf_ref, ys_hbm, o_ref, buf, sem):
    i = pl.program_id(0)
    n_tiles = pl.num_programs(0)
    rows = x_ref.shape[0]

    def issue(pref, slot):
        def body(r, carry):
            for k in range(TOP_K):
                p = pref[0, r * TOP_K + k]
                pltpu.make_async_copy(ys_hbm.at[pl.ds(p, 1), :], buf.at[slot, k, pl.ds(r, 1), :],
                                      sem.at[slot]).start()
            return carry

        lax.fori_loop(0, rows, body, 0, unroll=4)

    slot = i % 2

    @pl.when(i == 0)
    def _():
        issue(pos_ref, 0)

    @pl.when(i + 1 < n_tiles)
    def _():
        issue(posn_ref, 1 - slot)

    for k in range(TOP_K):
        pltpu.make_async_copy(ys_hbm.at[pl.ds(0, rows), :], buf.at[slot, k], sem.at[slot]).wait()
    w = w_ref[...]
    acc = w[:, 0:1] * buf[slot, 0]
    for k in range(1, TOP_K):
        acc = acc + w[:, k:k + 1] * buf[slot, k]
    o_ref[...] = x_ref[...] + gf_ref[...] * acc


def _combine(x, ys, pos_tiles, wts, tile_cond, gf):
    n, d = x.shape
    tm = TOKEN_TILE
    n_tiles = n // tm
    pos_spec = lambda f: pl.BlockSpec((None, 1, tm * TOP_K), f, memory_space=pltpu.SMEM)
    return pl.pallas_call(
        _combine_kernel,
        grid_spec=pltpu.PrefetchScalarGridSpec(
            num_scalar_prefetch=1,
            grid=(n_tiles,),
            in_specs=[
                pos_spec(lambda i, tc: (i, 0, 0)),
                pos_spec(lambda i, tc: (jnp.minimum(i + 1, n_tiles - 1), 0, 0)),
                pl.BlockSpec((tm, d), lambda i, tc: (i, 0)),
                pl.BlockSpec((tm, LANES), lambda i, tc: (i, 0)),
                pl.BlockSpec((None, 1, d), lambda i, tc: (tc[i], 0, 0)),
                pl.BlockSpec(memory_space=pl.ANY),
            ],
            out_specs=pl.BlockSpec((tm, d), lambda i, tc: (i, 0)),
            scratch_shapes=[
                pltpu.VMEM((2, TOP_K, tm, d), F32),
                pltpu.SemaphoreType.DMA((2,)),
            ],
        ),
        out_shape=jax.ShapeDtypeStruct((n, d), F32),
        compiler_params=_cparams(("arbitrary",)),
        name="moe_combine",
    )(tile_cond, pos_tiles, pos_tiles, x, wts, gf, ys)


def _moe(x, h2, meta, wts, counts, tile_cond, gf, w_gu, b_gu, w_down, b_down, layer):
    n, d = x.shape
    n_exp = w_gu.shape[1]
    rows_padded = n * TOP_K + n_exp * MOE_SUB
    pos, items = _moe_plan(meta, counts, n_exp, rows_padded)
    pos_tiles = pos.reshape(n // TOKEN_TILE, 1, TOKEN_TILE * TOP_K)
    xs = _dispatch(h2, pos_tiles, rows_padded)
    ys = _moe_experts(xs, items, w_gu, b_gu, w_down, b_down, layer)
    return _combine(x, ys, pos_tiles, wts, tile_cond, gf)


def _tile_cond(ns, npr, seq, tm):
    return jnp.asarray(np.concatenate([1 + np.arange(ns // tm) // (seq // tm),
                                       np.zeros(npr // tm, np.int64)]).astype(np.int32))


def kernel(x_prompt, x_sample, cache_na_k, cache_na_v, state_lru, c, c_ctx, norm_mix, norm_ffn, w_ada, b_ada, ev_w_in, ev_w_out, na_q_norm, na_k_norm, na_rpb, lru_conv_w, lru_conv_b, lru_wa, lru_ba, lru_wx, lru_bx, lru_lambda, od_w_in, od_w_out, sgu_norm, sgu_w, sgu_b, router_w, router_b, moe_w_gu, moe_b_gu, moe_w_down, moe_b_down):
    bp, sp_len, d = x_prompt.shape
    bs, ss_len, _ = x_sample.shape
    depth = w_ada.shape[0]
    heads, dh = cache_na_k.shape[3], cache_na_k.shape[4]
    na_w = heads * dh
    lru_w = state_lru.shape[-1]
    n_exp = router_w.shape[-1]
    ns, npr = bs * ss_len, bp * sp_len
    assert bs == REC_SEQS and bp % REC_SEQS == 0
    assert ss_len % ROW_TILE == 0 and npr % ROW_TILE == 0 and sp_len % TOKEN_TILE == 0

    x = jnp.concatenate([x_sample.reshape(ns, d), x_prompt.reshape(npr, d)], axis=0)
    tc_proj = _tile_cond(ns, npr, ss_len, ROW_TILE)
    tc_tok = _tile_cond(ns, npr, ss_len, TOKEN_TILE)
    cond_rows = 16
    cond = jnp.concatenate([c_ctx[None, :], c, jnp.zeros((cond_rows - 1 - bs, d), F32)], axis=0)
    mod = _adaln(cond, w_ada, b_ada)
    mod = mod.reshape(depth, cond_rows, 6, d).transpose(0, 2, 1, 3).reshape(depth, 6, cond_rows, 1, d)

    new_k, new_v, new_h = [], [], []
    for layer in range(depth):
        sm, scm, gm, sf, scf, gf = [mod[layer, i] for i in range(6)]
        g_mix = norm_mix[layer][None, :]
        g_ffn = norm_ffn[layer][None, :]
        if layer % 2 == 0:
            e = layer // 2
            ev_in = ev_w_in.shape[-1]
            head_gain = jnp.concatenate([jnp.tile(na_q_norm[e], heads), jnp.tile(na_k_norm[e], heads),
                                         jnp.ones((ev_in - 2 * na_w,), F32)])[None, :]
            proj = _inproj(x, tc_proj, g_mix, scm, sm, ev_w_in[e].astype(BF16), head_gain,
                           mode="heads", n_norm_cols=2 * na_w)
            att_s = _na_attention(proj, cache_na_k[:, e].reshape(bs, -1, na_w),
                                  cache_na_v[:, e].reshape(bs, -1, na_w), _na_bias_table(na_rpb[e]),
                                  nseq=bs, seq=ss_len, heads=heads, dh=dh)
            att_p = _ctx_attention(proj, row0=ns, nseq=bp, seq=sp_len, heads=heads, dh=dh)
            wgate, bgate = _pack_gate_weights(lru_wa[e], lru_ba[e], lru_wx[e], lru_bx[e])
            rec_args = (lru_conv_w[e], lru_conv_b[e][None, :], wgate, bgate, lru_lambda[e])
            h0_s = state_lru[:, e].transpose(1, 0, 2)[None]
            rec_s, _ = _recurrent(proj, *rec_args, h0_s, row0=0, nseq=bs, seq=ss_len,
                                  xr_col=3 * na_w, yg_col=3 * na_w + lru_w, width=lru_w)
            h0_p = jnp.zeros((bp // REC_SEQS, 2, REC_SEQS, lru_w), F32)
            rec_p, hl = _recurrent(proj, *rec_args, h0_p, row0=ns, nseq=bp, seq=sp_len,
                                   xr_col=3 * na_w, yg_col=3 * na_w + lru_w, width=lru_w)
            new_k.append(proj[ns:, na_w:2 * na_w].reshape(bp, sp_len, heads, dh))
            new_v.append(proj[ns:, 2 * na_w:3 * na_w].reshape(bp, sp_len, heads, dh))
            new_h.append(hl.transpose(0, 2, 1, 3).reshape(bp, 2, lru_w))
            att = jnp.concatenate([att_s, att_p], axis=0)
            rec = jnp.concatenate([rec_s, rec_p], axis=0)
            w_out = ev_w_out[e].astype(BF16)
            lhs_list, w_list = [att, rec], [w_out[:na_w], w_out[na_w:]]
        else:
            o = layer // 2
            z = _inproj(x, tc_proj, g_mix, scm, sm, od_w_in[o].astype(BF16),
                        jnp.ones((1, od_w_in.shape[-1]), F32), mode="gelu", n_norm_cols=0)
            chunk = sgu_w.shape[-1]
            sg = _sgu(z, sgu_norm[o][None, :], sgu_w[o].astype(BF16), sgu_b[o].T, chunk=chunk)
            lhs_list, w_list = [sg], [od_w_out[o].astype(BF16)]
        rw = jnp.pad(router_w[layer], ((0, 0), (0, LANES - n_exp)))
        rw_hi, rw_lo = _split_bf16(rw)
        rb = jnp.pad(router_b[layer], (0, LANES - n_exp))[None, :]
        x, h2, meta, wts, counts = _outproj(lhs_list, w_list, x, tc_proj, gm, g_ffn, scf, sf,
                                             rw_hi, rw_lo, rb, n_experts=n_exp)
        x = _moe(x, h2, meta, wts, counts, tc_tok, gf, moe_w_gu, moe_b_gu, moe_w_down, moe_b_down, layer)

    y_sample = x[:ns].reshape(bs, ss_len, d)
    y_prompt = x[ns:].reshape(bp, sp_len, d)
    new_na_k = jnp.stack(new_k, axis=1)
    new_na_v = jnp.stack(new_v, axis=1)
    new_state_lru = jnp.stack(new_h, axis=1)
    return (y_prompt, y_sample, new_na_k, new_na_v, new_state_lru)
```

```python
import functools

import numpy as np
import jax
import jax.numpy as jnp
from jax import lax
from jax.experimental import pallas as pl
from jax.experimental.pallas import tpu as pltpu

F32 = jnp.float32
BF16 = jnp.bfloat16

EPS = 1e-6
NEG_INF = -1e30
GRID_W = 64
NA_KH = 8
NA_KW = 16
LRU_C = 8.0
SWIGLU_ALPHA = 1.702
SWIGLU_LIMIT = 7.0
TOP_K = 4

LANES = 128
ROW_TILE = 512
PROJ_CHUNK = 256
TOKEN_TILE = 256
MOE_SUB = 256
MOE_ITEM_SUBS = 7
MOE_TF = 256
MOE_TN = 512
MOE_CAST_SLICES = (MOE_ITEM_SUBS + 1) // 2
VMEM_LIMIT = 56 * 1024 * 1024


def _cparams(sem, vmem=VMEM_LIMIT):
    return pltpu.CompilerParams(dimension_semantics=sem, vmem_limit_bytes=vmem)


def _split_bf16(x):
    hi = x.astype(BF16)
    lo = (x - hi.astype(F32)).astype(BF16)
    return hi, lo


def _gelu_tanh(x):
    return 0.5 * x * (1.0 + jnp.tanh(0.7978845608028654 * (x + 0.044715 * x * x * x)))


def _dot(a, b):
    return jnp.dot(a, b, preferred_element_type=F32)


def _dot_nt(a, b):
    return lax.dot_general(a, b, (((1,), (1,)), ((), ())), preferred_element_type=F32)


def _adaln_kernel(c_ref, w_ref, b_ref, o_ref):
    cnd = c_ref[...]
    s = cnd * jax.nn.sigmoid(cnd)
    hi, lo = _split_bf16(s)
    w = w_ref[...].astype(BF16)
    o_ref[...] = _dot(hi, w) + _dot(lo, w) + b_ref[...]


def _adaln(cond, w_ada, b_ada, tn=1024):
    depth, d, n6 = w_ada.shape
    rows = cond.shape[0]
    return pl.pallas_call(
        _adaln_kernel,
        grid=(depth, n6 // tn),
        in_specs=[
            pl.BlockSpec((rows, d), lambda l, j: (0, 0)),
            pl.BlockSpec((None, d, tn), lambda l, j: (l, 0, j)),
            pl.BlockSpec((None, 1, tn), lambda l, j: (l, 0, j)),
        ],
        out_specs=pl.BlockSpec((None, rows, tn), lambda l, j: (l, 0, j)),
        out_shape=jax.ShapeDtypeStruct((depth, rows, n6), F32),
        compiler_params=_cparams(("arbitrary", "arbitrary")),
        name="adaln",
    )(cond, w_ada, b_ada.reshape(depth, 1, n6))


def _modulated_norm(x, g, sc, sh):
    ms = jnp.mean(x * x, axis=-1, keepdims=True)
    return (x * lax.rsqrt(ms + EPS) * g) * (1.0 + sc) + sh


def _inproj_kernel(tc_ref, x_ref, g_ref, sc_ref, sh_ref, w_ref, hg_ref, o_ref, h_scr, *,
                   mode, n_norm_tiles):
    j = pl.program_id(1)

    @pl.when(j == 0)
    def _():
        h_scr[...] = _modulated_norm(x_ref[...], g_ref[...], sc_ref[...], sh_ref[...]).astype(BF16)

    def chunks(epilogue):
        for c in range(o_ref.shape[1] // PROJ_CHUNK):
            cs = slice(c * PROJ_CHUNK, (c + 1) * PROJ_CHUNK)
            o_ref[:, cs] = epilogue(_dot(h_scr[...], w_ref[:, cs]), cs)

    def head_norm(acc, cs):
        hg = hg_ref[:, cs]
        parts = []
        for c in range(PROJ_CHUNK // LANES):
            blk = acc[:, c * LANES:(c + 1) * LANES]
            ms = jnp.mean(blk * blk, axis=-1, keepdims=True)
            parts.append(blk * lax.rsqrt(ms + EPS) * hg[:, c * LANES:(c + 1) * LANES])
        return jnp.concatenate(parts, axis=1)

    if mode == "gelu":
        chunks(lambda acc, cs: _gelu_tanh(acc))
    else:
        @pl.when(j < n_norm_tiles)
        def _():
            chunks(head_norm)

        @pl.when(j >= n_norm_tiles)
        def _():
            chunks(lambda acc, cs: acc)


def _inproj(x, tile_cond, norm_g, sc, sh, w_bf16, head_gain, *, mode, n_norm_cols, tn=1024):
    n, d = x.shape
    nout = w_bf16.shape[1]
    tm = ROW_TILE
    kern = functools.partial(_inproj_kernel, mode=mode, n_norm_tiles=n_norm_cols // tn)
    modspec = pl.BlockSpec((None, 1, d), lambda i, j, tc: (tc[i], 0, 0))
    return pl.pallas_call(
        kern,
        grid_spec=pltpu.PrefetchScalarGridSpec(
            num_scalar_prefetch=1,
            grid=(n // tm, nout // tn),
            in_specs=[
                pl.BlockSpec((tm, d), lambda i, j, tc: (i, 0)),
                pl.BlockSpec((1, d), lambda i, j, tc: (0, 0)),
                modspec, modspec,
                pl.BlockSpec((d, tn), lambda i, j, tc: (0, j)),
                pl.BlockSpec((1, tn), lambda i, j, tc: (0, j)),
            ],
            out_specs=pl.BlockSpec((tm, tn), lambda i, j, tc: (i, j)),
            scratch_shapes=[pltpu.VMEM((tm, d), BF16)],
        ),
        out_shape=jax.ShapeDtypeStruct((n, nout), F32),
        compiler_params=_cparams(("arbitrary", "arbitrary")),
        name="inproj_" + mode,
    )(tile_cond, x, norm_g, sc, sh, w_bf16, head_gain)


def _ctx_attn_kernel(q_ref, k_ref, v_ref, o_ref, *, heads, dh):
    scale = dh ** -0.5
    for h in range(heads):
        sl = slice(h * dh, (h + 1) * dh)
        q = q_ref[:, sl].astype(BF16)
        k = k_ref[:, sl].astype(BF16)
        v = v_ref[:, sl].astype(BF16)
        s = _dot_nt(q, k) * scale
        m = jnp.max(s, axis=-1, keepdims=True)
        p = jnp.exp(s - m)
        den = jnp.sum(p, axis=-1, keepdims=True)
        o = _dot(p.astype(BF16), v) / den
        o_ref[:, sl] = o.astype(o_ref.dtype)


def _ctx_attention(proj, *, row0, nseq, seq, heads, dh):
    width = heads * dh
    rb0 = row0 // seq
    kern = functools.partial(_ctx_attn_kernel, heads=heads, dh=dh)
    return pl.pallas_call(
        kern,
        grid=(nseq,),
        in_specs=[
            pl.BlockSpec((seq, width), lambda b: (rb0 + b, 0)),
            pl.BlockSpec((seq, width), lambda b: (rb0 + b, 1)),
            pl.BlockSpec((seq, width), lambda b: (rb0 + b, 2)),
        ],
        out_specs=pl.BlockSpec((seq, width), lambda b: (b, 0)),
        out_shape=jax.ShapeDtypeStruct((nseq * seq, width), BF16),
        compiler_params=_cparams(("arbitrary",)),
        name="ctx_attention",
    )(proj, proj, proj)


def _na_attn_kernel(q_ref, k_ref, v_ref, kc_ref, vc_ref, bias_ref, o_ref, *, rows, dh):
    scale = dh ** -0.5
    win = NA_KH * GRID_W
    qc = lax.broadcasted_iota(jnp.int32, (GRID_W, win), 0)
    kc = lax.broadcasted_iota(jnp.int32, (GRID_W, win), 1) % GRID_W
    c_start = jnp.clip(qc - NA_KW // 2, 0, GRID_W - NA_KW)
    col_ok = (kc >= c_start) & (kc < c_start + NA_KW)
    k_ctx = kc_ref[...].astype(BF16)
    v_ctx = vc_ref[...].astype(BF16)
    for r in range(rows):
        rs = min(max(r - NA_KH // 2, 0), rows - NA_KH)
        q = q_ref[r * GRID_W:(r + 1) * GRID_W, :].astype(BF16)
        kw = k_ref[rs * GRID_W:rs * GRID_W + win, :].astype(BF16)
        vw = v_ref[rs * GRID_W:rs * GRID_W + win, :].astype(BF16)
        s_win = _dot_nt(q, kw) * scale + bias_ref[r - rs]
        s_win = jnp.where(col_ok, s_win, NEG_INF)
        s_ctx = _dot_nt(q, k_ctx) * scale
        m = jnp.maximum(jnp.max(s_win, axis=-1, keepdims=True),
                        jnp.max(s_ctx, axis=-1, keepdims=True))
        p_win = jnp.exp(s_win - m)
        p_ctx = jnp.exp(s_ctx - m)
        den = jnp.sum(p_win, axis=-1, keepdims=True) + jnp.sum(p_ctx, axis=-1, keepdims=True)
        o = (_dot(p_win.astype(BF16), vw) + _dot(p_ctx.astype(BF16), v_ctx)) / den
        o_ref[r * GRID_W:(r + 1) * GRID_W, :] = o.astype(o_ref.dtype)


def _na_bias_kernel(rpb_ref, o_ref, *, n_rows, n_cols):
    h = pl.program_id(0)
    q = lax.broadcasted_iota(jnp.int32, (GRID_W, GRID_W), 0)
    k = lax.broadcasted_iota(jnp.int32, (GRID_W, GRID_W), 1)
    col_idx = jnp.clip(k - q + NA_KW - 1, 0, n_cols - 1)
    for ri in range(n_rows):
        acc = jnp.zeros((GRID_W, GRID_W), F32)
        for j in range(n_cols):
            acc = jnp.where(col_idx == j, rpb_ref[(h * n_rows + ri) * n_cols + j], acc)
        o_ref[ri] = acc


def _na_bias_table(rpb):
    heads, n_rows, n_cols = rpb.shape
    toep = pl.pallas_call(
        functools.partial(_na_bias_kernel, n_rows=n_rows, n_cols=n_cols),
        grid=(heads,),
        in_specs=[pl.BlockSpec(memory_space=pltpu.SMEM)],
        out_specs=pl.BlockSpec((None, n_rows, GRID_W, GRID_W), lambda h: (h, 0, 0, 0)),
        out_shape=jax.ShapeDtypeStruct((heads, n_rows, GRID_W, GRID_W), F32),
        compiler_params=_cparams(("arbitrary",)),
        name="na_bias",
    )(rpb.reshape(-1))
    classes = []
    for d in range(NA_KH):
        t = toep[:, NA_KH - 1 - d:2 * NA_KH - 1 - d]
        classes.append(t.transpose(0, 2, 1, 3).reshape(heads, GRID_W, NA_KH * GRID_W))
    return jnp.stack(classes, axis=1)


def _na_attention(proj, cache_k, cache_v, bias_tab, *, nseq, seq, heads, dh):
    past = cache_k.shape[1]
    rows = seq // GRID_W
    kern = functools.partial(_na_attn_kernel, rows=rows, dh=dh)
    return pl.pallas_call(
        kern,
        grid=(heads, nseq),
        in_specs=[
            pl.BlockSpec((seq, dh), lambda h, b: (b, h)),
            pl.BlockSpec((seq, dh), lambda h, b: (b, heads + h)),
            pl.BlockSpec((seq, dh), lambda h, b: (b, 2 * heads + h)),
            pl.BlockSpec((None, past, dh), lambda h, b: (b, 0, h)),
            pl.BlockSpec((None, past, dh), lambda h, b: (b, 0, h)),
            pl.BlockSpec((None, NA_KH, GRID_W, NA_KH * GRID_W), lambda h, b: (h, 0, 0, 0)),
        ],
        out_specs=pl.BlockSpec((seq, dh), lambda h, b: (b, h)),
        out_shape=jax.ShapeDtypeStruct((nseq * seq, heads * dh), BF16),
        compiler_params=_cparams(("arbitrary", "arbitrary")),
        name="na_attention",
    )(proj, proj, proj, cache_k, cache_v, bias_tab)


REC_SEQS = 8
REC_CHUNK = 256


def _rec_kernel(xr_ref, yg_ref, cw_ref, cb_ref, wg_ref, bg_ref, lam_ref, h0_ref,
                out_ref, hl_ref, a_scr, b_scr, *, seq):
    pitch = seq + 8
    total = REC_SEQS * seq
    nchunks = total // REC_CHUNK
    row = lax.broadcasted_iota(jnp.int32, (REC_CHUNK, LANES), 0)
    cw = cw_ref[...]
    cb = cb_ref[...]
    wg_hi = wg_ref[0]
    wg_lo = wg_ref[1]
    bg = bg_ref[...]
    lam = lam_ref[...]
    sp = jnp.maximum(-lam, 0.0) + jnp.log(1.0 + jnp.exp(-jnp.abs(lam)))

    def coeff_chunk(ci, carry):
        r0 = pl.multiple_of(ci * REC_CHUNK, REC_CHUNK)
        t0 = r0 % seq
        x = xr_ref[pl.ds(r0, REC_CHUNK), :]
        prev = xr_ref[pl.ds(pl.multiple_of(jnp.maximum(r0 - 8, 0), 8), 8), :]
        nxt = xr_ref[pl.ds(pl.multiple_of(jnp.minimum(r0 + REC_CHUNK, total - 8), 8), 8), :]
        has_prev = (t0 > 0).astype(F32)
        has_next = (t0 + REC_CHUNK < seq).astype(F32)
        p6 = prev[6:7, :] * has_prev
        p7 = prev[7:8, :] * has_prev
        n0 = nxt[0:1, :] * has_next
        x_m1 = jnp.where(row == 0, p7, pltpu.roll(x, 1, axis=0))
        x_m2 = jnp.where(row == 0, p6, jnp.where(row == 1, p7, pltpu.roll(x, 2, axis=0)))
        x_p1 = jnp.where(row == REC_CHUNK - 1, n0, pltpu.roll(x, REC_CHUNK - 1, axis=0))
        xc = cb + x_m2 * cw[0:1, :] + x_m1 * cw[1:2, :] + x * cw[2:3, :] + x_p1 * cw[3:4, :]
        xh, xl = _split_bf16(xc)
        gates = _dot(xh, wg_hi) + _dot(xl, wg_hi) + _dot(xh, wg_lo) + bg
        srow = pl.multiple_of((r0 // seq) * pitch + t0, 8)
        for d in range(2):
            r = jax.nn.sigmoid(gates[:, (2 * d) * LANES:(2 * d + 1) * LANES])
            i = jax.nn.sigmoid(gates[:, (2 * d + 1) * LANES:(2 * d + 2) * LANES])
            log_a = (-LRU_C) * r * sp[d:d + 1, :]
            a = jnp.exp(log_a)
            b = jnp.sqrt(1.0 - jnp.exp(2.0 * log_a)) * (i * xc)
            a_scr[d, pl.ds(srow, REC_CHUNK), :] = a
            b_scr[d, pl.ds(srow, REC_CHUNK), :] = b
        return carry

    lax.fori_loop(0, nchunks, coeff_chunk, 0)

    def scan_step(t, carry):
        hf, hb = carry
        tb = seq - 1 - t
        hf = a_scr[0, pl.ds(t, REC_SEQS, stride=pitch), :] * hf + b_scr[0, pl.ds(t, REC_SEQS, stride=pitch), :]
        b_scr[0, pl.ds(t, REC_SEQS, stride=pitch), :] = hf
        hb = a_scr[1, pl.ds(tb, REC_SEQS, stride=pitch), :] * hb + b_scr[1, pl.ds(tb, REC_SEQS, stride=pitch), :]
        b_scr[1, pl.ds(tb, REC_SEQS, stride=pitch), :] = hb
        return hf, hb

    hf, hb = lax.fori_loop(0, seq, scan_step, (h0_ref[0], h0_ref[1]), unroll=8)
    hl_ref[0] = hf
    hl_ref[1] = hb

    def out_chunk(ci, carry):
        r0 = pl.multiple_of(ci * REC_CHUNK, REC_CHUNK)
        srow = pl.multiple_of((r0 // seq) * pitch + r0 % seq, 8)
        hsum = b_scr[0, pl.ds(srow, REC_CHUNK), :] + b_scr[1, pl.ds(srow, REC_CHUNK), :]
        out_ref[pl.ds(r0, REC_CHUNK), :] = (_gelu_tanh(yg_ref[pl.ds(r0, REC_CHUNK), :]) * hsum).astype(out_ref.dtype)
        return carry

    lax.fori_loop(0, nchunks, out_chunk, 0)


def _recurrent(proj, conv_w, conv_b, wgate, bgate, lam, h0, *, row0, nseq, seq, xr_col, yg_col, width):
    groups = nseq // REC_SEQS
    blk_rows = REC_SEQS * seq
    rb0 = row0 // blk_rows
    nblk = width // LANES
    xc0 = xr_col // LANES
    yc0 = yg_col // LANES
    pitch = seq + 8
    kern = functools.partial(_rec_kernel, seq=seq)
    return pl.pallas_call(
        kern,
        grid=(groups, nblk),
        in_specs=[
            pl.BlockSpec((blk_rows, LANES), lambda g, n: (rb0 + g, xc0 + n)),
            pl.BlockSpec((blk_rows, LANES), lambda g, n: (rb0 + g, yc0 + n)),
            pl.BlockSpec((4, LANES), lambda g, n: (0, n)),
            pl.BlockSpec((1, LANES), lambda g, n: (0, n)),
            pl.BlockSpec((None, 2, LANES, 4 * LANES), lambda g, n: (n, 0, 0, 0)),
            pl.BlockSpec((None, 1, 4 * LANES), lambda g, n: (n, 0, 0)),
            pl.BlockSpec((2, LANES), lambda g, n: (0, n)),
            pl.BlockSpec((None, 2, REC_SEQS, LANES), lambda g, n: (g, 0, 0, n)),
        ],
        out_specs=[
            pl.BlockSpec((blk_rows, LANES), lambda g, n: (g, n)),
            pl.BlockSpec((None, 2, REC_SEQS, LANES), lambda g, n: (g, 0, 0, n)),
        ],
        out_shape=[
            jax.ShapeDtypeStruct((nseq * seq, width), BF16),
            jax.ShapeDtypeStruct((groups, 2, REC_SEQS, width), F32),
        ],
        scratch_shapes=[
            pltpu.VMEM((2, REC_SEQS * pitch, LANES), F32),
            pltpu.VMEM((2, REC_SEQS * pitch, LANES), F32),
        ],
        compiler_params=_cparams(("arbitrary", "arbitrary")),
        name="recurrent",
    )(proj, proj, conv_w, conv_b, wgate, bgate, lam, h0)


def _pack_gate_weights(wa, ba, wx, bx):
    nblk = wa.shape[1]
    w = jnp.concatenate([wa[0], wx[0], wa[1], wx[1]], axis=-1)
    hi, lo = _split_bf16(w)
    b = jnp.stack([ba[0].reshape(nblk, LANES), bx[0].reshape(nblk, LANES),
                   ba[1].reshape(nblk, LANES), bx[1].reshape(nblk, LANES)], axis=1)
    return jnp.stack([hi, lo], axis=1), b.reshape(nblk, 1, 4 * LANES)


def _sgu_kernel(u_ref, v_ref, g_ref, ws_ref, bs_ref, o_ref, *, chunk, groups):
    g = g_ref[...]
    for c in range(u_ref.shape[0] // chunk):
        rs = slice(c * chunk, (c + 1) * chunk)
        v = v_ref[rs, :]
        ms = jnp.mean(v * v, axis=-1, keepdims=True)
        vn = (v * lax.rsqrt(ms + EPS) * g).astype(BF16)
        for grp in range(groups):
            cs = slice(grp * LANES, (grp + 1) * LANES)
            mixed = _dot(ws_ref[grp], vn[:, cs]) + bs_ref[:, grp:grp + 1]
            o_ref[rs, cs] = (u_ref[rs, cs] * mixed).astype(o_ref.dtype)


def _sgu(z, v_norm, ws_bf16, bs_t, *, chunk, tm=512):
    n, w2 = z.shape
    w = w2 // 2
    groups = ws_bf16.shape[0]
    kern = functools.partial(_sgu_kernel, chunk=chunk, groups=groups)
    return pl.pallas_call(
        kern,
        grid=(n // tm,),
        in_specs=[
            pl.BlockSpec((tm, w), lambda i: (i, 0)),
            pl.BlockSpec((tm, w), lambda i: (i, 1)),
            pl.BlockSpec((1, w), lambda i: (0, 0)),
            pl.BlockSpec((groups, chunk, chunk), lambda i: (0, 0, 0)),
            pl.BlockSpec((chunk, groups), lambda i: (0, 0)),
        ],
        out_specs=pl.BlockSpec((tm, w), lambda i: (i, 0)),
        out_shape=jax.ShapeDtypeStruct((n, w), BF16),
        compiler_params=_cparams(("arbitrary",)),
        name="sgu",
    )(z, z, v_norm, ws_bf16, bs_t)


def _outproj_kernel(tc_ref, *refs, n_lhs, n_experts):
    lhs = refs[:n_lhs]
    ws = refs[n_lhs:2 * n_lhs]
    (x_ref, gm_ref, g2_ref, scf_ref, sf_ref, rwh_ref, rwl_ref, rb_ref, tri_ref,
     xo_ref, h2_ref, meta_ref, wts_ref, cnt_ref, carry) = refs[2 * n_lhs:]
    @pl.when(pl.program_id(0) == 0)
    def _():
        carry[...] = jnp.zeros(carry.shape, carry.dtype)

    count = carry[...]
    chunk = tri_ref.shape[0]
    for rc in range(x_ref.shape[0] // chunk):
        rows = slice(rc * chunk, (rc + 1) * chunk)
        mix = _dot(lhs[0][rows, :], ws[0][...])
        for a_ref, w_ref in zip(lhs[1:], ws[1:]):
            mix = mix + _dot(a_ref[rows, :], w_ref[...])
        xn = x_ref[rows, :] + gm_ref[...] * mix
        xo_ref[rows, :] = xn
        h2 = _modulated_norm(xn, g2_ref[...], scf_ref[...], sf_ref[...])
        hi, lo = _split_bf16(h2)
        h2_ref[rows, :] = h2
        logits = _dot(hi, rwh_ref[...]) + _dot(lo, rwh_ref[...]) + _dot(hi, rwl_ref[...]) + rb_ref[...]
        lane = lax.broadcasted_iota(jnp.int32, logits.shape, 1)
        l = jnp.where(lane < n_experts, logits, -jnp.inf)
        vals, idxs = [], []
        for _ in range(TOP_K):
            m = jnp.max(l, axis=-1, keepdims=True)
            i = jnp.min(jnp.where(l == m, lane, LANES), axis=-1, keepdims=True)
            vals.append(m)
            idxs.append(i)
            l = jnp.where(lane == i, -jnp.inf, l)
        es = [jnp.exp(v - vals[0]) for v in vals]
        den = es[0] + es[1] + es[2] + es[3]
        chosen = jnp.zeros(logits.shape, F32)
        for k in range(TOP_K):
            chosen = chosen + (lane == idxs[k]).astype(F32)
        before = count + _dot(tri_ref[...], chosen.astype(BF16))
        meta = jnp.zeros(logits.shape, jnp.int32)
        wts_out = jnp.zeros(logits.shape, F32)
        for k in range(TOP_K):
            rank = jnp.sum(jnp.where(lane == idxs[k], before, 0.0), axis=-1, keepdims=True).astype(jnp.int32)
            meta = jnp.where(lane == k, idxs[k], meta)
            meta = jnp.where(lane == TOP_K + k, rank, meta)
            wts_out = jnp.where(lane == k, es[k] / den, wts_out)
        meta_ref[rows, :] = meta
        wts_ref[rows, :] = wts_out
        count = count + jnp.sum(chosen, axis=0, keepdims=True)
    carry[...] = count
    cnt_ref[...] = count


def _outproj(lhs_list, w_list, x, tile_cond, gm, g2, scf, sf, rw_hi, rw_lo, rb, *, n_experts):
    n, d = x.shape
    tm = ROW_TILE
    n_lhs = len(lhs_list)
    kern = functools.partial(_outproj_kernel, n_lhs=n_lhs, n_experts=n_experts)
    modspec = pl.BlockSpec((None, 1, d), lambda i, tc: (tc[i], 0, 0))
    tri = jnp.asarray(np.tril(np.ones((tm, tm), np.float32), -1), BF16)
    in_specs = (
        [pl.BlockSpec((tm, a.shape[1]), lambda i, tc: (i, 0)) for a in lhs_list]
        + [pl.BlockSpec(w.shape, lambda i, tc: (0, 0)) for w in w_list]
        + [pl.BlockSpec((tm, d), lambda i, tc: (i, 0)),
           modspec,
           pl.BlockSpec((1, d), lambda i, tc: (0, 0)),
           modspec, modspec,
           pl.BlockSpec((d, LANES), lambda i, tc: (0, 0)),
           pl.BlockSpec((d, LANES), lambda i, tc: (0, 0)),
           pl.BlockSpec((1, LANES), lambda i, tc: (0, 0)),
           pl.BlockSpec((tm, tm), lambda i, tc: (0, 0))])
    return pl.pallas_call(
        kern,
        grid_spec=pltpu.PrefetchScalarGridSpec(
            num_scalar_prefetch=1,
            grid=(n // tm,),
            in_specs=in_specs,
            out_specs=[
                pl.BlockSpec((tm, d), lambda i, tc: (i, 0)),
                pl.BlockSpec((tm, d), lambda i, tc: (i, 0)),
                pl.BlockSpec((tm, LANES), lambda i, tc: (i, 0)),
                pl.BlockSpec((tm, LANES), lambda i, tc: (i, 0)),
                pl.BlockSpec((1, LANES), lambda i, tc: (0, 0)),
            ],
            scratch_shapes=[pltpu.VMEM((1, LANES), F32)],
        ),
        out_shape=[
            jax.ShapeDtypeStruct((n, d), F32),
            jax.ShapeDtypeStruct((n, d), F32),
            jax.ShapeDtypeStruct((n, LANES), jnp.int32),
            jax.ShapeDtypeStruct((n, LANES), F32),
            jax.ShapeDtypeStruct((1, LANES), F32),
        ],
        compiler_params=_cparams(("arbitrary",)),
        name="outproj_router",
    )(tile_cond, *lhs_list, *w_list, x, gm, g2, scf, sf, rw_hi, rw_lo, rb, tri)


def _dispatch_kernel(zt_ref, pos_ref, h_ref, xs_hbm, zbuf, sem, zsem):
    rows = h_ref.shape[0]

    @pl.when(pl.program_id(0) == 0)
    def _():
        zbuf[...] = jnp.zeros(zbuf.shape, zbuf.dtype)

        def zero_tile(i):
            return pltpu.make_async_copy(zbuf, xs_hbm.at[pl.ds(zt_ref[i] * MOE_SUB, MOE_SUB), :], zsem)

        for i in range(zt_ref.shape[0]):
            @pl.when(zt_ref[i] >= 0)
            def _():
                zero_tile(i).start()
        for i in range(zt_ref.shape[0]):
            @pl.when(zt_ref[i] >= 0)
            def _():
                zero_tile(i).wait()

    def issue(r, carry):
        for k in range(TOP_K):
            p = pos_ref[0, r * TOP_K + k]
            pltpu.make_async_copy(h_ref.at[pl.ds(r, 1), :], xs_hbm.at[pl.ds(p, 1), :], sem).start()
        return carry

    lax.fori_loop(0, rows, issue, 0, unroll=4)
    for k in range(TOP_K):
        pltpu.make_async_copy(h_ref, xs_hbm.at[pl.ds(0, rows), :], sem).wait()


def _dispatch(h2, pos_tiles, zero_tiles, rows_padded):
    n, d = h2.shape
    tm = TOKEN_TILE
    return pl.pallas_call(
        _dispatch_kernel,
        grid_spec=pltpu.PrefetchScalarGridSpec(
            num_scalar_prefetch=1,
            grid=(n // tm,),
            in_specs=[
                pl.BlockSpec((None, 1, tm * TOP_K), lambda i, zt: (i, 0, 0), memory_space=pltpu.SMEM),
                pl.BlockSpec((tm, d), lambda i, zt: (i, 0)),
            ],
            out_specs=pl.BlockSpec(memory_space=pl.ANY),
            scratch_shapes=[
                pltpu.VMEM((MOE_SUB, d), h2.dtype),
                pltpu.SemaphoreType.DMA,
                pltpu.SemaphoreType.DMA,
            ],
        ),
        out_shape=jax.ShapeDtypeStruct((rows_padded, d), h2.dtype),
        compiler_params=_cparams(("arbitrary",)),
        name="moe_dispatch",
    )(zero_tiles, pos_tiles, h2)


def _moe_kernel(ie_ref, it0_ref, int_ref, inz_ref, xs_hbm, wgu_hbm, wd_hbm, bgu_ref, bd_ref,
                ys_hbm, stage, xb, act, out, wf, wb, sem_x, sem_w, sem_out, *, layer, n1, n2, f):
    s = pl.program_id(0)
    n_items = pl.num_programs(0)
    nch = n1 + n2
    nt = int_ref[s]
    nz = inz_ref[s]
    t0 = it0_ref[s]
    e = ie_ref[s]
    sn = jnp.minimum(s + 1, n_items - 1)
    has_next = (s + 1 < n_items) & (int_ref[sn] > 0)
    e_next = ie_ref[sn]
    t0_next = it0_ref[sn]
    nt_next = int_ref[sn]
    cast_rows = wf.shape[1] // MOE_CAST_SLICES

    def w_start(ex, kk, slot):
        @pl.when(kk < n1)
        def _():
            col = pl.multiple_of(kk * MOE_TF, MOE_TF)
            pltpu.make_async_copy(wgu_hbm.at[layer, ex, :, pl.ds(col, MOE_TF)],
                                  wf.at[slot, :, pl.ds(0, MOE_TF)], sem_w.at[slot]).start()
            pltpu.make_async_copy(wgu_hbm.at[layer, ex, :, pl.ds(f + col, MOE_TF)],
                                  wf.at[slot, :, pl.ds(MOE_TF, MOE_TF)], sem_w.at[slot]).start()

        @pl.when(kk >= n1)
        def _():
            col = pl.multiple_of((kk - n1) * MOE_TN, MOE_TN)
            pltpu.make_async_copy(wd_hbm.at[layer, ex, :, pl.ds(col, MOE_TN)], wf.at[slot], sem_w.at[slot]).start()

    def w_wait(slot):
        pltpu.make_async_copy(wd_hbm.at[layer, 0, :, pl.ds(0, MOE_TN)], wf.at[slot], sem_w.at[slot]).wait()

    def cast_slice(slot, q):
        r0 = q * cast_rows if isinstance(q, int) else pl.multiple_of(q * cast_rows, cast_rows)
        wb[slot, pl.ds(r0, cast_rows), :] = wf[slot, pl.ds(r0, cast_rows), :].astype(BF16)

    def x_copy(tile, slot):
        return pltpu.make_async_copy(xs_hbm.at[pl.ds(tile * MOE_SUB, MOE_SUB), :], stage.at[slot], sem_x.at[slot])

    def x_land(slot, j):
        r0 = pl.multiple_of(j * MOE_SUB, MOE_SUB)
        xb[pl.ds(r0, MOE_SUB), :] = stage[slot].astype(BF16)

    def copy_out(n, j):
        col = n * MOE_TN if isinstance(n, int) else pl.multiple_of(n * MOE_TN, MOE_TN)
        return pltpu.make_async_copy(
            out.at[n, pl.ds(j * MOE_SUB, MOE_SUB), :],
            ys_hbm.at[pl.ds((t0 + j) * MOE_SUB, MOE_SUB), pl.ds(col, MOE_TN)],
            sem_out.at[n])

    def sub_tiles(fn, other):
        def pair(p, carry):
            fn(2 * p)
            fn(2 * p + 1)
            cast_slice(other, p)
            return carry

        lax.fori_loop(0, nt // 2, pair, 0)

        @pl.when(nt % 2 == 1)
        def _():
            fn(nt - 1)
            cast_slice(other, nt // 2)

        trips = nt // 2 + nt % 2
        for q in range(MOE_CAST_SLICES):
            @pl.when(q >= trips)
            def _():
                cast_slice(other, q)

    def advance(k):
        slot = k % 2
        k2 = k + 2

        @pl.when(k2 < nch)
        def _():
            w_start(e, k2, slot)

        @pl.when((k2 >= nch) & has_next)
        def _():
            w_start(e_next, k2 - nch, slot)

        @pl.when((k + 1 < nch) | has_next)
        def _():
            w_wait(1 - slot)
        return slot

    @pl.when(nz > 0)
    def _():
        out[0, pl.ds(0, MOE_SUB), :] = jnp.zeros((MOE_SUB, MOE_TN), out.dtype)

        def copy_zero(n, j):
            return pltpu.make_async_copy(
                out.at[0, pl.ds(0, MOE_SUB), :],
                ys_hbm.at[pl.ds((t0 + j) * MOE_SUB, MOE_SUB), pl.ds(n * MOE_TN, MOE_TN)], sem_out.at[n])

        for j in range(MOE_ITEM_SUBS):
            @pl.when(j < nz)
            def _():
                for n in range(n2):
                    copy_zero(n, j).start()
        for j in range(MOE_ITEM_SUBS):
            @pl.when(j < nz)
            def _():
                for n in range(n2):
                    copy_zero(n, j).wait()

    @pl.when((s == 0) & (nt > 0))
    def _():
        w_start(e, 0, 0)
        w_start(e, 1, 1)
        x_copy(t0, 0).start()
        for j in range(MOE_ITEM_SUBS):
            @pl.when(j < nt)
            def _():
                if j + 1 < MOE_ITEM_SUBS:
                    @pl.when(j + 1 < nt)
                    def _():
                        x_copy(t0 + j + 1, (j + 1) % 2).start()
                x_copy(t0 + j, j % 2).wait()
                x_land(j % 2, j)
        w_wait(0)
        for q in range(MOE_CAST_SLICES):
            cast_slice(0, q)

    @pl.when(nt > 0)
    def _():
        def up_step(k, carry):
            slot = advance(k)
            bg = bgu_ref[k]
            bl = bgu_ref[n1 + k]

            def up(j):
                r0 = pl.multiple_of(j * MOE_SUB, MOE_SUB)
                x = xb[pl.ds(r0, MOE_SUB), :]
                glu = jnp.minimum(_dot(x, wb[slot, :, pl.ds(0, MOE_TF)]) + bg, SWIGLU_LIMIT)
                lin = jnp.clip(_dot(x, wb[slot, :, pl.ds(MOE_TF, MOE_TF)]) + bl, -SWIGLU_LIMIT, SWIGLU_LIMIT)
                act[k, pl.ds(r0, MOE_SUB), :] = (glu * jax.nn.sigmoid(SWIGLU_ALPHA * glu) * (lin + 1.0)).astype(BF16)

            sub_tiles(up, 1 - slot)
            return carry

        lax.fori_loop(0, n1, up_step, 0)

        def down_step(n, carry):
            slot = advance(n1 + n)
            bd = bd_ref[n]
            for jj in range(2):
                @pl.when(has_next & (2 * n + jj < nt_next))
                def _():
                    x_copy(t0_next + 2 * n + jj, jj).start()

            def down(j):
                r0 = pl.multiple_of(j * MOE_SUB, MOE_SUB)
                a = jnp.concatenate([act[k, pl.ds(r0, MOE_SUB), :] for k in range(n1)], axis=1)
                out[n, pl.ds(r0, MOE_SUB), :] = _dot(a, wb[slot]) + bd

            sub_tiles(down, 1 - slot)
            for j in range(MOE_ITEM_SUBS):
                @pl.when(j < nt)
                def _():
                    copy_out(n, j).start()
            for jj in range(2):
                @pl.when(has_next & (2 * n + jj < nt_next))
                def _():
                    x_copy(t0_next + 2 * n + jj, jj).wait()
                    x_land(jj, 2 * n + jj)
            return carry

        lax.fori_loop(0, n2, down_step, 0)
        for n in range(n2):
            for j in range(MOE_ITEM_SUBS):
                @pl.when(j < nt)
                def _():
                    copy_out(n, j).wait()


def _moe_experts(xs, items, w_gu, b_gu, w_down, b_down, layer):
    rp, d = xs.shape
    n_layers, n_exp, _, f2 = w_gu.shape
    f = f2 // 2
    assert f == d and MOE_TN == 2 * MOE_TF and MOE_ITEM_SUBS <= 2 * (d // MOE_TN)
    assert d % MOE_CAST_SLICES == 0
    n1 = f // MOE_TF
    n2 = d // MOE_TN
    item_e, item_t0, item_nt, item_nz = items
    n_items = item_e.shape[0]
    tm = MOE_ITEM_SUBS * MOE_SUB
    kern = functools.partial(_moe_kernel, layer=layer, n1=n1, n2=n2, f=f)
    return pl.pallas_call(
        kern,
        grid_spec=pltpu.PrefetchScalarGridSpec(
            num_scalar_prefetch=4,
            grid=(n_items,),
            in_specs=[
                pl.BlockSpec(memory_space=pl.ANY),
                pl.BlockSpec(memory_space=pl.ANY),
                pl.BlockSpec(memory_space=pl.ANY),
                pl.BlockSpec((None, None, 2 * n1, 1, MOE_TF), lambda s, ie, it0, nt, nz: (layer, ie[s], 0, 0, 0)),
                pl.BlockSpec((None, None, n2, 1, MOE_TN), lambda s, ie, it0, nt, nz: (layer, ie[s], 0, 0, 0)),
            ],
            out_specs=pl.BlockSpec(memory_space=pl.ANY),
            scratch_shapes=[
                pltpu.VMEM((2, MOE_SUB, d), F32),
                pltpu.VMEM((tm, d), BF16),
                pltpu.VMEM((n1, tm, MOE_TF), BF16),
                pltpu.VMEM((n2, tm, MOE_TN), F32),
                pltpu.VMEM((2, d, MOE_TN), F32),
                pltpu.VMEM((2, d, MOE_TN), BF16),
                pltpu.SemaphoreType.DMA((2,)),
                pltpu.SemaphoreType.DMA((2,)),
                pltpu.SemaphoreType.DMA((n2,)),
            ],
        ),
        out_shape=jax.ShapeDtypeStruct((rp, d), F32),
        compiler_params=_cparams(("arbitrary",)),
        name="moe_experts",
    )(item_e, item_t0, item_nt, item_nz, xs, w_gu, w_down,
      b_gu.reshape(n_layers, n_exp, 2 * n1, 1, MOE_TF), b_down.reshape(n_layers, n_exp, n2, 1, MOE_TN))


def _moe_plan(meta, counts, n_experts, rows_padded):
    n = meta.shape[0]
    top_i = meta[:, :TOP_K]
    rank = meta[:, TOP_K:2 * TOP_K]
    counts = counts[0, :n_experts].astype(jnp.int32)
    ntiles = (counts + MOE_SUB - 1) // MOE_SUB
    tile_end = jnp.cumsum(ntiles)
    tile_start = tile_end - ntiles
    experts = jnp.arange(n_experts, dtype=jnp.int32)
    start_sel = jnp.sum(jnp.where(top_i[:, :, None] == experts[None, None, :], tile_start[None, None, :], 0), axis=-1)
    pos = start_sel * MOE_SUB + rank
    n_items_max = n_experts + (n * TOP_K // MOE_SUB + MOE_ITEM_SUBS - 1) // MOE_ITEM_SUBS
    items_e = (ntiles + MOE_ITEM_SUBS - 1) // MOE_ITEM_SUBS
    item_end = jnp.cumsum(items_e)
    item_start = item_end - items_e
    sidx = jnp.arange(n_items_max, dtype=jnp.int32)
    valid = sidx < item_end[-1]
    ie = jnp.sum((sidx[:, None] >= item_end[None, :]).astype(jnp.int32), axis=1)
    ie = jnp.minimum(ie, n_experts - 1)
    pick = ie[:, None] == experts[None, :]
    local = sidx - jnp.sum(jnp.where(pick, item_start[None, :], 0), axis=1)
    t0 = jnp.sum(jnp.where(pick, tile_start[None, :], 0), axis=1) + local * MOE_ITEM_SUBS
    nt = jnp.clip(jnp.sum(jnp.where(pick, ntiles[None, :], 0), axis=1) - local * MOE_ITEM_SUBS, 0, MOE_ITEM_SUBS)
    last_e = jnp.max(jnp.where(valid, ie, 0))
    ie = jnp.where(valid, ie, last_e).astype(jnp.int32)
    z0 = tile_end[-1] + (sidx - item_end[-1]) * MOE_ITEM_SUBS
    nz = jnp.clip(rows_padded // MOE_SUB - z0, 0, MOE_ITEM_SUBS)
    t0 = jnp.where(valid, t0, z0).astype(jnp.int32)
    nt = jnp.where(valid, nt, 0).astype(jnp.int32)
    nz = jnp.where(valid, 0, nz).astype(jnp.int32)
    total_tiles = rows_padded // MOE_SUB
    tail = tile_end[-1] + jnp.arange(total_tiles - n * TOP_K // MOE_SUB, dtype=jnp.int32)
    zero_tiles = jnp.concatenate([jnp.where(ntiles > 0, tile_end - 1, -1),
                                  jnp.where(tail < total_tiles, tail, -1)]).astype(jnp.int32)
    return pos.astype(jnp.int32), zero_tiles, (ie, t0, nt, nz)


def _combine_kernel(tc_ref, pos_ref, posn_ref, x_ref, w_ref, gf_ref, ys_hbm, *rest, split_tiles):
    o_refs, (buf, sem) = rest[:-2], rest[-2:]
    i = pl.program_id(0)
    n_tiles = pl.num_programs(0)
    rows = x_ref.shape[0]

    def issue(pref, slot):
        def body(r, carry):
            for k in range(TOP_K):
                p = pref[0, r * TOP_K + k]
                pltpu.make_async_copy(ys_hbm.at[pl.ds(p, 1), :], buf.at[slot, k, pl.ds(r, 1), :],
                                      sem.at[slot]).start()
            return carry

        lax.fori_loop(0, rows, body, 0, unroll=4)

    slot = i % 2

    @pl.when(i == 0)
    def _():
        issue(pos_ref, 0)

    @pl.when(i + 1 < n_tiles)
    def _():
        issue(posn_ref, 1 - slot)

    for k in range(TOP_K):
        pltpu.make_async_copy(ys_hbm.at[pl.ds(0, rows), :], buf.at[slot, k], sem.at[slot]).wait()
    w = w_ref[...]
    acc = w[:, 0:1] * buf[slot, 0]
    for k in range(1, TOP_K):
        acc = acc + w[:, k:k + 1] * buf[slot, k]
    val = x_ref[...] + gf_ref[...] * acc
    if split_tiles is None:
        o_refs[0][...] = val
    else:
        @pl.when(i < split_tiles)
        def _():
            o_refs[0][...] = val

        @pl.when(i >= split_tiles)
        def _():
            o_refs[1][...] = val


def _combine(x, ys, pos_tiles, wts, tile_cond, gf, split_rows=None):
    n, d = x.shape
    tm = TOKEN_TILE
    n_tiles = n // tm
    pos_spec = lambda f: pl.BlockSpec((None, 1, tm * TOP_K), f, memory_space=pltpu.SMEM)
    if split_rows is None:
        split_tiles = None
        out_specs = pl.BlockSpec((tm, d), lambda i, tc: (i, 0))
        out_shape = jax.ShapeDtypeStruct((n, d), F32)
    else:
        split_tiles = split_rows // tm
        out_specs = [pl.BlockSpec((tm, d), lambda i, tc: (jnp.minimum(i, split_tiles - 1), 0)),
                     pl.BlockSpec((tm, d), lambda i, tc: (jnp.maximum(i - split_tiles, 0), 0))]
        out_shape = [jax.ShapeDtypeStruct((split_rows, d), F32), jax.ShapeDtypeStruct((n - split_rows, d), F32)]
    return pl.pallas_call(
        functools.partial(_combine_kernel, split_tiles=split_tiles),
        grid_spec=pltpu.PrefetchScalarGridSpec(
            num_scalar_prefetch=1,
            grid=(n_tiles,),
            in_specs=[
                pos_spec(lambda i, tc: (i, 0, 0)),
                pos_spec(lambda i, tc: (jnp.minimum(i + 1, n_tiles - 1), 0, 0)),
                pl.BlockSpec((tm, d), lambda i, tc: (i, 0)),
                pl.BlockSpec((tm, LANES), lambda i, tc: (i, 0)),
                pl.BlockSpec((None, 1, d), lambda i, tc: (tc[i], 0, 0)),
                pl.BlockSpec(memory_space=pl.ANY),
            ],
            out_specs=out_specs,
            scratch_shapes=[
                pltpu.VMEM((2, TOP_K, tm, d), F32),
                pltpu.SemaphoreType.DMA((2,)),
            ],
        ),
        out_shape=out_shape,
        compiler_params=_cparams(("arbitrary",)),
        name="moe_combine",
    )(tile_cond, pos_tiles, pos_tiles, x, wts, gf, ys)


def _moe(x, h2, meta, wts, counts, tile_cond, gf, w_gu, b_gu, w_down, b_down, layer, split_rows=None):
    n, d = x.shape
    n_exp = w_gu.shape[1]
    rows_padded = n * TOP_K + n_exp * MOE_SUB
    pos, zero_tiles, items = _moe_plan(meta, counts, n_exp, rows_padded)
    pos_tiles = pos.reshape(n // TOKEN_TILE, 1, TOKEN_TILE * TOP_K)
    xs = _dispatch(h2, pos_tiles, zero_tiles, rows_padded)
    ys = _moe_experts(xs, items, w_gu, b_gu, w_down, b_down, layer)
    return _combine(x, ys, pos_tiles, wts, tile_cond, gf, split_rows)


def _tile_cond(ns, npr, seq, tm):
    return jnp.asarray(np.concatenate([1 + np.arange(ns // tm) // (seq // tm),
                                       np.zeros(npr // tm, np.int64)]).astype(np.int32))


def kernel(x_prompt, x_sample, cache_na_k, cache_na_v, state_lru, c, c_ctx, norm_mix, norm_ffn, w_ada, b_ada, ev_w_in, ev_w_out, na_q_norm, na_k_norm, na_rpb, lru_conv_w, lru_conv_b, lru_wa, lru_ba, lru_wx, lru_bx, lru_lambda, od_w_in, od_w_out, sgu_norm, sgu_w, sgu_b, router_w, router_b, moe_w_gu, moe_b_gu, moe_w_down, moe_b_down):
    bp, sp_len, d = x_prompt.shape
    bs, ss_len, _ = x_sample.shape
    depth = w_ada.shape[0]
    heads, dh = cache_na_k.shape[3], cache_na_k.shape[4]
    na_w = heads * dh
    lru_w = state_lru.shape[-1]
    n_exp = router_w.shape[-1]
    ns, npr = bs * ss_len, bp * sp_len
    assert bs == REC_SEQS and bp % REC_SEQS == 0
    assert ss_len % ROW_TILE == 0 and npr % ROW_TILE == 0 and sp_len % TOKEN_TILE == 0

    x = jnp.concatenate([x_sample.reshape(ns, d), x_prompt.reshape(npr, d)], axis=0)
    tc_proj = _tile_cond(ns, npr, ss_len, ROW_TILE)
    tc_tok = _tile_cond(ns, npr, ss_len, TOKEN_TILE)
    cond_rows = 16
    cond = jnp.concatenate([c_ctx[None, :], c, jnp.zeros((cond_rows - 1 - bs, d), F32)], axis=0)
    mod = _adaln(cond, w_ada, b_ada)
    mod = mod.reshape(depth, cond_rows, 6, d).transpose(0, 2, 1, 3).reshape(depth, 6, cond_rows, 1, d)

    new_k, new_v, new_h = [], [], []
    for layer in range(depth):
        sm, scm, gm, sf, scf, gf = [mod[layer, i] for i in range(6)]
        g_mix = norm_mix[layer][None, :]
        g_ffn = norm_ffn[layer][None, :]
        if layer % 2 == 0:
            e = layer // 2
            ev_in = ev_w_in.shape[-1]
            head_gain = jnp.concatenate([jnp.tile(na_q_norm[e], heads), jnp.tile(na_k_norm[e], heads),
                                         jnp.ones((ev_in - 2 * na_w,), F32)])[None, :]
            proj = _inproj(x, tc_proj, g_mix, scm, sm, ev_w_in[e].astype(BF16), head_gain,
                           mode="heads", n_norm_cols=2 * na_w)
            att_s = _na_attention(proj, cache_na_k[:, e].reshape(bs, -1, na_w),
                                  cache_na_v[:, e].reshape(bs, -1, na_w), _na_bias_table(na_rpb[e]),
                                  nseq=bs, seq=ss_len, heads=heads, dh=dh)
            att_p = _ctx_attention(proj, row0=ns, nseq=bp, seq=sp_len, heads=heads, dh=dh)
            wgate, bgate = _pack_gate_weights(lru_wa[e], lru_ba[e], lru_wx[e], lru_bx[e])
            rec_args = (lru_conv_w[e], lru_conv_b[e][None, :], wgate, bgate, lru_lambda[e])
            h0_s = state_lru[:, e].transpose(1, 0, 2)[None]
            rec_s, _ = _recurrent(proj, *rec_args, h0_s, row0=0, nseq=bs, seq=ss_len,
                                  xr_col=3 * na_w, yg_col=3 * na_w + lru_w, width=lru_w)
            h0_p = jnp.zeros((bp // REC_SEQS, 2, REC_SEQS, lru_w), F32)
            rec_p, hl = _recurrent(proj, *rec_args, h0_p, row0=ns, nseq=bp, seq=sp_len,
                                   xr_col=3 * na_w, yg_col=3 * na_w + lru_w, width=lru_w)
            new_k.append(proj[ns:, na_w:2 * na_w].reshape(bp, sp_len, heads, dh))
            new_v.append(proj[ns:, 2 * na_w:3 * na_w].reshape(bp, sp_len, heads, dh))
            new_h.append(hl.transpose(0, 2, 1, 3).reshape(bp, 2, lru_w))
            att = jnp.concatenate([att_s, att_p], axis=0)
            rec = jnp.concatenate([rec_s, rec_p], axis=0)
            w_out = ev_w_out[e].astype(BF16)
            lhs_list, w_list = [att, rec], [w_out[:na_w], w_out[na_w:]]
        else:
            o = layer // 2
            z = _inproj(x, tc_proj, g_mix, scm, sm, od_w_in[o].astype(BF16),
                        jnp.ones((1, od_w_in.shape[-1]), F32), mode="gelu", n_norm_cols=0)
            chunk = sgu_w.shape[-1]
            sg = _sgu(z, sgu_norm[o][None, :], sgu_w[o].astype(BF16), sgu_b[o].T, chunk=chunk)
            lhs_list, w_list = [sg], [od_w_out[o].astype(BF16)]
        rw = jnp.pad(router_w[layer], ((0, 0), (0, LANES - n_exp)))
        rw_hi, rw_lo = _split_bf16(rw)
        rb = jnp.pad(router_b[layer], (0, LANES - n_exp))[None, :]
        x, h2, meta, wts, counts = _outproj(lhs_list, w_list, x, tc_proj, gm, g_ffn, scf, sf,
                                             rw_hi, rw_lo, rb, n_experts=n_exp)
        x = _moe(x, h2, meta, wts, counts, tc_tok, gf, moe_w_gu, moe_b_gu, moe_w_down, moe_b_down, layer,
                 split_rows=ns if layer == depth - 1 else None)

    y_sample = x[0].reshape(bs, ss_len, d)
    y_prompt = x[1].reshape(bp, sp_len, d)
    new_na_k = jnp.stack(new_k, axis=1)
    new_na_v = jnp.stack(new_v, axis=1)
    new_state_lru = jnp.stack(new_h, axis=1)
    return (y_prompt, y_sample, new_na_k, new_na_v, new_state_lru)
```

```python
import functools

import numpy as np
import jax
import jax.numpy as jnp
from jax import lax
from jax.experimental import pallas as pl
from jax.experimental.pallas import tpu as pltpu

F32 = jnp.float32
BF16 = jnp.bfloat16

EPS = 1e-6
NEG_INF = -1e30
GRID_W = 64
NA_KH = 8
NA_KW = 16
LRU_C = 8.0
SWIGLU_ALPHA = 1.702
SWIGLU_LIMIT = 7.0
TOP_K = 4

LANES = 128
INPROJ_TILE = 1024
ROW_TILE = 512
PROJ_CHUNK = 256
TOKEN_TILE = 256
MOE_SUB = 256
MOE_ITEM_SUBS = 7
MOE_TF = 256
MOE_TN = 512
MOE_GROUP = 3
MOE_CAST_SLICES = 4
VMEM_LIMIT = 56 * 1024 * 1024


def _cparams(sem, vmem=VMEM_LIMIT):
    return pltpu.CompilerParams(dimension_semantics=sem, vmem_limit_bytes=vmem)


def _split_bf16(x):
    hi = x.astype(BF16)
    lo = (x - hi.astype(F32)).astype(BF16)
    return hi, lo


def _gelu_tanh(x):
    return 0.5 * x * (1.0 + jnp.tanh(0.7978845608028654 * (x + 0.044715 * x * x * x)))


def _dot(a, b):
    return jnp.dot(a, b, preferred_element_type=F32)


def _dot_nt(a, b):
    return lax.dot_general(a, b, (((1,), (1,)), ((), ())), preferred_element_type=F32)


def _adaln_kernel(c_ref, w_ref, b_ref, o_ref):
    cnd = c_ref[...]
    s = cnd * jax.nn.sigmoid(cnd)
    hi, lo = _split_bf16(s)
    w = w_ref[...].astype(BF16)
    o_ref[...] = _dot(hi, w) + _dot(lo, w) + b_ref[...]


def _adaln(cond, w_ada, b_ada, tn=1024):
    depth, d, n6 = w_ada.shape
    rows = cond.shape[0]
    return pl.pallas_call(
        _adaln_kernel,
        grid=(depth, n6 // tn),
        in_specs=[
            pl.BlockSpec((rows, d), lambda l, j: (0, 0)),
            pl.BlockSpec((None, d, tn), lambda l, j: (l, 0, j)),
            pl.BlockSpec((None, 1, tn), lambda l, j: (l, 0, j)),
        ],
        out_specs=pl.BlockSpec((None, rows, tn), lambda l, j: (l, 0, j)),
        out_shape=jax.ShapeDtypeStruct((depth, rows, n6), F32),
        compiler_params=_cparams(("arbitrary", "arbitrary")),
        name="adaln",
    )(cond, w_ada, b_ada.reshape(depth, 1, n6))


def _modulated_norm(x, g, sc, sh):
    ms = jnp.mean(x * x, axis=-1, keepdims=True)
    return (x * lax.rsqrt(ms + EPS) * g) * (1.0 + sc) + sh


def _row_specs(parts, tm, width, **spec_kwargs):
    specs, bounds, off = [], [], 0
    for a in parts:
        nt = a.shape[0] // tm
        specs.append(pl.BlockSpec((tm, width), lambda i, *_, off=off, nt=nt: (jnp.clip(i - off, 0, nt - 1), 0),
                                  **spec_kwargs))
        off += nt
        bounds.append(off)
    return specs, tuple(bounds)


def _row_select(refs, bounds, i, idx):
    val = refs[-1][idx]
    for p in range(len(refs) - 2, -1, -1):
        val = jnp.where(i < bounds[p], refs[p][idx], val)
    return val


def _inproj_kernel(tc_ref, *refs, mode, n_norm_tiles, bounds):
    x_refs = refs[:len(bounds)]
    g_ref, sc_ref, sh_ref, w_ref, hg_ref, o_ref, h_scr = refs[len(bounds):]
    i = pl.program_id(0)
    j = pl.program_id(1)

    @pl.when(j == 0)
    def _():
        for rc in range(h_scr.shape[0] // PROJ_CHUNK):
            rows = slice(rc * PROJ_CHUNK, (rc + 1) * PROJ_CHUNK)
            x = _row_select(x_refs, bounds, i, (rows, slice(None)))
            h_scr[rows, :] = _modulated_norm(x, g_ref[...], sc_ref[...], sh_ref[...]).astype(BF16)

    def chunks(epilogue):
        for c in range(o_ref.shape[1] // PROJ_CHUNK):
            cs = slice(c * PROJ_CHUNK, (c + 1) * PROJ_CHUNK)
            o_ref[:, cs] = epilogue(_dot(h_scr[...], w_ref[:, cs]), cs)

    def head_norm(acc, cs):
        hg = hg_ref[:, cs]
        parts = []
        for c in range(PROJ_CHUNK // LANES):
            blk = acc[:, c * LANES:(c + 1) * LANES]
            ms = jnp.mean(blk * blk, axis=-1, keepdims=True)
            parts.append(blk * lax.rsqrt(ms + EPS) * hg[:, c * LANES:(c + 1) * LANES])
        return jnp.concatenate(parts, axis=1)

    if mode == "gelu":
        chunks(lambda acc, cs: _gelu_tanh(acc))
    else:
        @pl.when(j < n_norm_tiles)
        def _():
            chunks(head_norm)

        @pl.when(j >= n_norm_tiles)
        def _():
            chunks(lambda acc, cs: acc)


def _inproj(x_parts, tile_cond, norm_g, sc, sh, w_bf16, head_gain, *, mode, n_norm_cols, tn=1024):
    n = sum(a.shape[0] for a in x_parts)
    d = x_parts[0].shape[1]
    nout = w_bf16.shape[1]
    tm = INPROJ_TILE
    x_specs, bounds = _row_specs(x_parts, tm, d, pipeline_mode=pl.Buffered(1))
    kern = functools.partial(_inproj_kernel, mode=mode, n_norm_tiles=n_norm_cols // tn, bounds=bounds)
    modspec = pl.BlockSpec((None, 1, d), lambda i, j, tc: (tc[i], 0, 0))
    return pl.pallas_call(
        kern,
        grid_spec=pltpu.PrefetchScalarGridSpec(
            num_scalar_prefetch=1,
            grid=(n // tm, nout // tn),
            in_specs=x_specs + [
                pl.BlockSpec((1, d), lambda i, j, tc: (0, 0)),
                modspec, modspec,
                pl.BlockSpec((d, tn), lambda i, j, tc: (0, j)),
                pl.BlockSpec((1, tn), lambda i, j, tc: (0, j)),
            ],
            out_specs=pl.BlockSpec((tm, tn), lambda i, j, tc: (i, j)),
            scratch_shapes=[pltpu.VMEM((tm, d), BF16)],
        ),
        out_shape=jax.ShapeDtypeStruct((n, nout), F32),
        compiler_params=_cparams(("arbitrary", "arbitrary")),
        name="inproj_" + mode,
    )(tile_cond, *x_parts, norm_g, sc, sh, w_bf16, head_gain)


def _ctx_attn_kernel(q_ref, k_ref, v_ref, o_ref, *, heads, dh):
    scale = dh ** -0.5
    for h in range(heads):
        sl = slice(h * dh, (h + 1) * dh)
        q = q_ref[:, sl].astype(BF16)
        k = k_ref[:, sl].astype(BF16)
        v = v_ref[:, sl].astype(BF16)
        s = _dot_nt(q, k) * scale
        m = jnp.max(s, axis=-1, keepdims=True)
        p = jnp.exp(s - m)
        den = jnp.sum(p, axis=-1, keepdims=True)
        o = _dot(p.astype(BF16), v) / den
        o_ref[:, sl] = o.astype(o_ref.dtype)


def _ctx_attention(proj, *, row0, nseq, seq, heads, dh):
    width = heads * dh
    rb0 = row0 // seq
    kern = functools.partial(_ctx_attn_kernel, heads=heads, dh=dh)
    return pl.pallas_call(
        kern,
        grid=(nseq,),
        in_specs=[
            pl.BlockSpec((seq, width), lambda b: (rb0 + b, 0)),
            pl.BlockSpec((seq, width), lambda b: (rb0 + b, 1)),
            pl.BlockSpec((seq, width), lambda b: (rb0 + b, 2)),
        ],
        out_specs=pl.BlockSpec((seq, width), lambda b: (b, 0)),
        out_shape=jax.ShapeDtypeStruct((nseq * seq, width), BF16),
        compiler_params=_cparams(("arbitrary",)),
        name="ctx_attention",
    )(proj, proj, proj)


def _na_attn_kernel(q_ref, k_ref, v_ref, kc_ref, vc_ref, bias_ref, o_ref, *, rows, dh):
    scale = dh ** -0.5
    win = NA_KH * GRID_W
    qc = lax.broadcasted_iota(jnp.int32, (GRID_W, win), 0)
    kc = lax.broadcasted_iota(jnp.int32, (GRID_W, win), 1) % GRID_W
    c_start = jnp.clip(qc - NA_KW // 2, 0, GRID_W - NA_KW)
    col_ok = (kc >= c_start) & (kc < c_start + NA_KW)
    k_ctx = kc_ref[...].astype(BF16)
    v_ctx = vc_ref[...].astype(BF16)
    for r in range(rows):
        rs = min(max(r - NA_KH // 2, 0), rows - NA_KH)
        q = q_ref[r * GRID_W:(r + 1) * GRID_W, :].astype(BF16)
        kw = k_ref[rs * GRID_W:rs * GRID_W + win, :].astype(BF16)
        vw = v_ref[rs * GRID_W:rs * GRID_W + win, :].astype(BF16)
        s_win = _dot_nt(q, kw) * scale + bias_ref[r - rs]
        s_win = jnp.where(col_ok, s_win, NEG_INF)
        s_ctx = _dot_nt(q, k_ctx) * scale
        m = jnp.maximum(jnp.max(s_win, axis=-1, keepdims=True),
                        jnp.max(s_ctx, axis=-1, keepdims=True))
        p_win = jnp.exp(s_win - m)
        p_ctx = jnp.exp(s_ctx - m)
        den = jnp.sum(p_win, axis=-1, keepdims=True) + jnp.sum(p_ctx, axis=-1, keepdims=True)
        o = (_dot(p_win.astype(BF16), vw) + _dot(p_ctx.astype(BF16), v_ctx)) / den
        o_ref[r * GRID_W:(r + 1) * GRID_W, :] = o.astype(o_ref.dtype)


def _na_bias_kernel(rpb_ref, o_ref, *, n_rows, n_cols):
    h = pl.program_id(0)
    q = lax.broadcasted_iota(jnp.int32, (GRID_W, GRID_W), 0)
    k = lax.broadcasted_iota(jnp.int32, (GRID_W, GRID_W), 1)
    col_idx = jnp.clip(k - q + NA_KW - 1, 0, n_cols - 1)
    for ri in range(n_rows):
        acc = jnp.zeros((GRID_W, GRID_W), F32)
        for j in range(n_cols):
            acc = jnp.where(col_idx == j, rpb_ref[(h * n_rows + ri) * n_cols + j], acc)
        o_ref[ri] = acc


def _na_bias_table(rpb):
    heads, n_rows, n_cols = rpb.shape
    toep = pl.pallas_call(
        functools.partial(_na_bias_kernel, n_rows=n_rows, n_cols=n_cols),
        grid=(heads,),
        in_specs=[pl.BlockSpec(memory_space=pltpu.SMEM)],
        out_specs=pl.BlockSpec((None, n_rows, GRID_W, GRID_W), lambda h: (h, 0, 0, 0)),
        out_shape=jax.ShapeDtypeStruct((heads, n_rows, GRID_W, GRID_W), F32),
        compiler_params=_cparams(("arbitrary",)),
        name="na_bias",
    )(rpb.reshape(-1))
    classes = []
    for d in range(NA_KH):
        t = toep[:, NA_KH - 1 - d:2 * NA_KH - 1 - d]
        classes.append(t.transpose(0, 2, 1, 3).reshape(heads, GRID_W, NA_KH * GRID_W))
    return jnp.stack(classes, axis=1)


def _na_attention(proj, cache_k, cache_v, bias_tab, *, nseq, seq, heads, dh):
    past = cache_k.shape[1]
    rows = seq // GRID_W
    kern = functools.partial(_na_attn_kernel, rows=rows, dh=dh)
    return pl.pallas_call(
        kern,
        grid=(heads, nseq),
        in_specs=[
            pl.BlockSpec((seq, dh), lambda h, b: (b, h)),
            pl.BlockSpec((seq, dh), lambda h, b: (b, heads + h)),
            pl.BlockSpec((seq, dh), lambda h, b: (b, 2 * heads + h)),
            pl.BlockSpec((None, past, dh), lambda h, b: (b, 0, h)),
            pl.BlockSpec((None, past, dh), lambda h, b: (b, 0, h)),
            pl.BlockSpec((None, NA_KH, GRID_W, NA_KH * GRID_W), lambda h, b: (h, 0, 0, 0)),
        ],
        out_specs=pl.BlockSpec((seq, dh), lambda h, b: (b, h)),
        out_shape=jax.ShapeDtypeStruct((nseq * seq, heads * dh), BF16),
        compiler_params=_cparams(("arbitrary", "arbitrary")),
        name="na_attention",
    )(proj, proj, proj, cache_k, cache_v, bias_tab)


REC_SEQS = 8
REC_CHUNK = 256


def _rec_kernel(xr_ref, yg_ref, cw_ref, cb_ref, wg_ref, bg_ref, lam_ref, h0_ref,
                out_ref, hl_ref, a_scr, b_scr, *, seq):
    pitch = seq + 8
    total = REC_SEQS * seq
    nchunks = total // REC_CHUNK
    row = lax.broadcasted_iota(jnp.int32, (REC_CHUNK, LANES), 0)
    cw = cw_ref[...]
    cb = cb_ref[...]
    wg_hi = wg_ref[0]
    wg_lo = wg_ref[1]
    bg = bg_ref[...]
    lam = lam_ref[...]
    sp = jnp.maximum(-lam, 0.0) + jnp.log(1.0 + jnp.exp(-jnp.abs(lam)))

    def coeff_chunk(ci, carry):
        r0 = pl.multiple_of(ci * REC_CHUNK, REC_CHUNK)
        t0 = r0 % seq
        x = xr_ref[pl.ds(r0, REC_CHUNK), :]
        prev = xr_ref[pl.ds(pl.multiple_of(jnp.maximum(r0 - 8, 0), 8), 8), :]
        nxt = xr_ref[pl.ds(pl.multiple_of(jnp.minimum(r0 + REC_CHUNK, total - 8), 8), 8), :]
        has_prev = (t0 > 0).astype(F32)
        has_next = (t0 + REC_CHUNK < seq).astype(F32)
        p6 = prev[6:7, :] * has_prev
        p7 = prev[7:8, :] * has_prev
        n0 = nxt[0:1, :] * has_next
        x_m1 = jnp.where(row == 0, p7, pltpu.roll(x, 1, axis=0))
        x_m2 = jnp.where(row == 0, p6, jnp.where(row == 1, p7, pltpu.roll(x, 2, axis=0)))
        x_p1 = jnp.where(row == REC_CHUNK - 1, n0, pltpu.roll(x, REC_CHUNK - 1, axis=0))
        xc = cb + x_m2 * cw[0:1, :] + x_m1 * cw[1:2, :] + x * cw[2:3, :] + x_p1 * cw[3:4, :]
        xh, xl = _split_bf16(xc)
        gates = _dot(xh, wg_hi) + _dot(xl, wg_hi) + _dot(xh, wg_lo) + bg
        srow = pl.multiple_of((r0 // seq) * pitch + t0, 8)
        for d in range(2):
            r = jax.nn.sigmoid(gates[:, (2 * d) * LANES:(2 * d + 1) * LANES])
            i = jax.nn.sigmoid(gates[:, (2 * d + 1) * LANES:(2 * d + 2) * LANES])
            log_a = (-LRU_C) * r * sp[d:d + 1, :]
            a = jnp.exp(log_a)
            b = jnp.sqrt(1.0 - jnp.exp(2.0 * log_a)) * (i * xc)
            a_scr[d, pl.ds(srow, REC_CHUNK), :] = a
            b_scr[d, pl.ds(srow, REC_CHUNK), :] = b
        return carry

    lax.fori_loop(0, nchunks, coeff_chunk, 0)

    def scan_step(t, carry):
        hf, hb = carry
        tb = seq - 1 - t
        hf = a_scr[0, pl.ds(t, REC_SEQS, stride=pitch), :] * hf + b_scr[0, pl.ds(t, REC_SEQS, stride=pitch), :]
        b_scr[0, pl.ds(t, REC_SEQS, stride=pitch), :] = hf
        hb = a_scr[1, pl.ds(tb, REC_SEQS, stride=pitch), :] * hb + b_scr[1, pl.ds(tb, REC_SEQS, stride=pitch), :]
        b_scr[1, pl.ds(tb, REC_SEQS, stride=pitch), :] = hb
        return hf, hb

    hf, hb = lax.fori_loop(0, seq, scan_step, (h0_ref[0], h0_ref[1]), unroll=8)
    hl_ref[0] = hf
    hl_ref[1] = hb

    def out_chunk(ci, carry):
        r0 = pl.multiple_of(ci * REC_CHUNK, REC_CHUNK)
        srow = pl.multiple_of((r0 // seq) * pitch + r0 % seq, 8)
        hsum = b_scr[0, pl.ds(srow, REC_CHUNK), :] + b_scr[1, pl.ds(srow, REC_CHUNK), :]
        out_ref[pl.ds(r0, REC_CHUNK), :] = (_gelu_tanh(yg_ref[pl.ds(r0, REC_CHUNK), :]) * hsum).astype(out_ref.dtype)
        return carry

    lax.fori_loop(0, nchunks, out_chunk, 0)


def _recurrent(proj, conv_w, conv_b, wgate, bgate, lam, h0, *, row0, nseq, seq, xr_col, yg_col, width):
    groups = nseq // REC_SEQS
    blk_rows = REC_SEQS * seq
    rb0 = row0 // blk_rows
    nblk = width // LANES
    xc0 = xr_col // LANES
    yc0 = yg_col // LANES
    pitch = seq + 8
    kern = functools.partial(_rec_kernel, seq=seq)
    return pl.pallas_call(
        kern,
        grid=(groups, nblk),
        in_specs=[
            pl.BlockSpec((blk_rows, LANES), lambda g, n: (rb0 + g, xc0 + n)),
            pl.BlockSpec((blk_rows, LANES), lambda g, n: (rb0 + g, yc0 + n)),
            pl.BlockSpec((4, LANES), lambda g, n: (0, n)),
            pl.BlockSpec((1, LANES), lambda g, n: (0, n)),
            pl.BlockSpec((None, 2, LANES, 4 * LANES), lambda g, n: (n, 0, 0, 0)),
            pl.BlockSpec((None, 1, 4 * LANES), lambda g, n: (n, 0, 0)),
            pl.BlockSpec((2, LANES), lambda g, n: (0, n)),
            pl.BlockSpec((None, 2, REC_SEQS, LANES), lambda g, n: (g, 0, 0, n)),
        ],
        out_specs=[
            pl.BlockSpec((blk_rows, LANES), lambda g, n: (g, n)),
            pl.BlockSpec((None, 2, REC_SEQS, LANES), lambda g, n: (g, 0, 0, n)),
        ],
        out_shape=[
            jax.ShapeDtypeStruct((nseq * seq, width), BF16),
            jax.ShapeDtypeStruct((groups, 2, REC_SEQS, width), F32),
        ],
        scratch_shapes=[
            pltpu.VMEM((2, REC_SEQS * pitch, LANES), F32),
            pltpu.VMEM((2, REC_SEQS * pitch, LANES), F32),
        ],
        compiler_params=_cparams(("arbitrary", "arbitrary")),
        name="recurrent",
    )(proj, proj, conv_w, conv_b, wgate, bgate, lam, h0)


def _pack_gate_weights(wa, ba, wx, bx):
    nblk = wa.shape[1]
    w = jnp.concatenate([wa[0], wx[0], wa[1], wx[1]], axis=-1)
    hi, lo = _split_bf16(w)
    b = jnp.stack([ba[0].reshape(nblk, LANES), bx[0].reshape(nblk, LANES),
                   ba[1].reshape(nblk, LANES), bx[1].reshape(nblk, LANES)], axis=1)
    return jnp.stack([hi, lo], axis=1), b.reshape(nblk, 1, 4 * LANES)


def _sgu_kernel(u_ref, v_ref, g_ref, ws_ref, bs_ref, o_ref, *, chunk, groups):
    g = g_ref[...]
    for c in range(u_ref.shape[0] // chunk):
        rs = slice(c * chunk, (c + 1) * chunk)
        v = v_ref[rs, :]
        ms = jnp.mean(v * v, axis=-1, keepdims=True)
        vn = (v * lax.rsqrt(ms + EPS) * g).astype(BF16)
        for grp in range(groups):
            cs = slice(grp * LANES, (grp + 1) * LANES)
            mixed = _dot(ws_ref[grp], vn[:, cs]) + bs_ref[:, grp:grp + 1]
            o_ref[rs, cs] = (u_ref[rs, cs] * mixed).astype(o_ref.dtype)


def _sgu(z, v_norm, ws_bf16, bs_t, *, chunk, tm=512):
    n, w2 = z.shape
    w = w2 // 2
    groups = ws_bf16.shape[0]
    kern = functools.partial(_sgu_kernel, chunk=chunk, groups=groups)
    return pl.pallas_call(
        kern,
        grid=(n // tm,),
        in_specs=[
            pl.BlockSpec((tm, w), lambda i: (i, 0)),
            pl.BlockSpec((tm, w), lambda i: (i, 1)),
            pl.BlockSpec((1, w), lambda i: (0, 0)),
            pl.BlockSpec((groups, chunk, chunk), lambda i: (0, 0, 0)),
            pl.BlockSpec((chunk, groups), lambda i: (0, 0)),
        ],
        out_specs=pl.BlockSpec((tm, w), lambda i: (i, 0)),
        out_shape=jax.ShapeDtypeStruct((n, w), BF16),
        compiler_params=_cparams(("arbitrary",)),
        name="sgu",
    )(z, z, v_norm, ws_bf16, bs_t)


def _outproj_kernel(tc_ref, *refs, lhs_bounds, x_bounds, n_experts):
    tile = pl.program_id(0)
    refs = list(refs)
    lhs = [[refs.pop(0) for _ in b] for b in lhs_bounds]
    ws = [refs.pop(0) for _ in lhs_bounds]
    x_refs = [refs.pop(0) for _ in x_bounds]
    (gm_ref, g2_ref, scf_ref, sf_ref, rwh_ref, rwl_ref, rb_ref, tri_ref,
     xo_ref, h2_ref, meta_ref, wts_ref, cnt_ref, carry) = refs
    @pl.when(pl.program_id(0) == 0)
    def _():
        carry[...] = jnp.zeros(carry.shape, carry.dtype)

    count = carry[...]
    chunk = tri_ref.shape[0]
    for rc in range(xo_ref.shape[0] // chunk):
        rows = slice(rc * chunk, (rc + 1) * chunk)
        idx = (rows, slice(None))
        mix = _dot(_row_select(lhs[0], lhs_bounds[0], tile, idx), ws[0][...])
        for parts, bounds, w_ref in zip(lhs[1:], lhs_bounds[1:], ws[1:]):
            mix = mix + _dot(_row_select(parts, bounds, tile, idx), w_ref[...])
        xn = _row_select(x_refs, x_bounds, tile, idx) + gm_ref[...] * mix
        xo_ref[rows, :] = xn
        h2 = _modulated_norm(xn, g2_ref[...], scf_ref[...], sf_ref[...])
        hi, lo = _split_bf16(h2)
        h2_ref[rows, :] = h2
        logits = _dot(hi, rwh_ref[...]) + _dot(lo, rwh_ref[...]) + _dot(hi, rwl_ref[...]) + rb_ref[...]
        lane = lax.broadcasted_iota(jnp.int32, logits.shape, 1)
        l = jnp.where(lane < n_experts, logits, -jnp.inf)
        vals, idxs = [], []
        for _ in range(TOP_K):
            m = jnp.max(l, axis=-1, keepdims=True)
            i = jnp.min(jnp.where(l == m, lane, LANES), axis=-1, keepdims=True)
            vals.append(m)
            idxs.append(i)
            l = jnp.where(lane == i, -jnp.inf, l)
        es = [jnp.exp(v - vals[0]) for v in vals]
        den = es[0] + es[1] + es[2] + es[3]
        chosen = jnp.zeros(logits.shape, F32)
        for k in range(TOP_K):
            chosen = chosen + (lane == idxs[k]).astype(F32)
        before = count + _dot(tri_ref[...], chosen.astype(BF16))
        meta = jnp.zeros(logits.shape, jnp.int32)
        wts_out = jnp.zeros(logits.shape, F32)
        for k in range(TOP_K):
            rank = jnp.sum(jnp.where(lane == idxs[k], before, 0.0), axis=-1, keepdims=True).astype(jnp.int32)
            meta = jnp.where(lane == k, idxs[k], meta)
            meta = jnp.where(lane == TOP_K + k, rank, meta)
            wts_out = jnp.where(lane == k, es[k] / den, wts_out)
        meta_ref[rows, :] = meta
        wts_ref[rows, :] = wts_out
        count = count + jnp.sum(chosen, axis=0, keepdims=True)
    carry[...] = count
    cnt_ref[...] = count


def _outproj(lhs_list, w_list, x_parts, tile_cond, gm, g2, scf, sf, rw_hi, rw_lo, rb, *, n_experts):
    n = sum(a.shape[0] for a in x_parts)
    d = x_parts[0].shape[1]
    tm = ROW_TILE
    lhs_specs, lhs_bounds = [], []
    for parts in lhs_list:
        specs, bounds = _row_specs(parts, tm, parts[0].shape[1])
        lhs_specs += specs
        lhs_bounds.append(bounds)
    x_specs, x_bounds = _row_specs(x_parts, tm, d)
    kern = functools.partial(_outproj_kernel, lhs_bounds=tuple(lhs_bounds), x_bounds=x_bounds, n_experts=n_experts)
    modspec = pl.BlockSpec((None, 1, d), lambda i, tc: (tc[i], 0, 0))
    tri = jnp.asarray(np.tril(np.ones((tm, tm), np.float32), -1), BF16)
    in_specs = (
        lhs_specs
        + [pl.BlockSpec(w.shape, lambda i, tc: (0, 0)) for w in w_list]
        + x_specs
        + [modspec,
           pl.BlockSpec((1, d), lambda i, tc: (0, 0)),
           modspec, modspec,
           pl.BlockSpec((d, LANES), lambda i, tc: (0, 0)),
           pl.BlockSpec((d, LANES), lambda i, tc: (0, 0)),
           pl.BlockSpec((1, LANES), lambda i, tc: (0, 0)),
           pl.BlockSpec((tm, tm), lambda i, tc: (0, 0))])
    return pl.pallas_call(
        kern,
        grid_spec=pltpu.PrefetchScalarGridSpec(
            num_scalar_prefetch=1,
            grid=(n // tm,),
            in_specs=in_specs,
            out_specs=[
                pl.BlockSpec((tm, d), lambda i, tc: (i, 0)),
                pl.BlockSpec((tm, d), lambda i, tc: (i, 0)),
                pl.BlockSpec((tm, LANES), lambda i, tc: (i, 0)),
                pl.BlockSpec((tm, LANES), lambda i, tc: (i, 0)),
                pl.BlockSpec((1, LANES), lambda i, tc: (0, 0)),
            ],
            scratch_shapes=[pltpu.VMEM((1, LANES), F32)],
        ),
        out_shape=[
            jax.ShapeDtypeStruct((n, d), F32),
            jax.ShapeDtypeStruct((n, d), F32),
            jax.ShapeDtypeStruct((n, LANES), jnp.int32),
            jax.ShapeDtypeStruct((n, LANES), F32),
            jax.ShapeDtypeStruct((1, LANES), F32),
        ],
        compiler_params=_cparams(("arbitrary",)),
        name="outproj_router",
    )(tile_cond, *[a for parts in lhs_list for a in parts], *w_list, *x_parts,
      gm, g2, scf, sf, rw_hi, rw_lo, rb, tri)


def _dispatch_kernel(zt_ref, pos_ref, h_ref, xs_hbm, zbuf, sem, zsem):
    rows = h_ref.shape[0]

    @pl.when(pl.program_id(0) == 0)
    def _():
        zbuf[...] = jnp.zeros(zbuf.shape, zbuf.dtype)

        def zero_tile(i):
            return pltpu.make_async_copy(zbuf, xs_hbm.at[pl.ds(zt_ref[i] * MOE_SUB, MOE_SUB), :], zsem)

        for i in range(zt_ref.shape[0]):
            @pl.when(zt_ref[i] >= 0)
            def _():
                zero_tile(i).start()
        for i in range(zt_ref.shape[0]):
            @pl.when(zt_ref[i] >= 0)
            def _():
                zero_tile(i).wait()

    def issue(r, carry):
        for k in range(TOP_K):
            p = pos_ref[0, r * TOP_K + k]
            pltpu.make_async_copy(h_ref.at[pl.ds(r, 1), :], xs_hbm.at[pl.ds(p, 1), :], sem).start()
        return carry

    lax.fori_loop(0, rows, issue, 0, unroll=4)
    for k in range(TOP_K):
        pltpu.make_async_copy(h_ref, xs_hbm.at[pl.ds(0, rows), :], sem).wait()


def _dispatch(h2, pos_tiles, zero_tiles, rows_padded):
    n, d = h2.shape
    tm = TOKEN_TILE
    return pl.pallas_call(
        _dispatch_kernel,
        grid_spec=pltpu.PrefetchScalarGridSpec(
            num_scalar_prefetch=1,
            grid=(n // tm,),
            in_specs=[
                pl.BlockSpec((None, 1, tm * TOP_K), lambda i, zt: (i, 0, 0), memory_space=pltpu.SMEM),
                pl.BlockSpec((tm, d), lambda i, zt: (i, 0)),
            ],
            out_specs=pl.BlockSpec(memory_space=pl.ANY),
            scratch_shapes=[
                pltpu.VMEM((MOE_SUB, d), h2.dtype),
                pltpu.SemaphoreType.DMA,
                pltpu.SemaphoreType.DMA,
            ],
        ),
        out_shape=jax.ShapeDtypeStruct((rows_padded, d), h2.dtype),
        compiler_params=_cparams(("arbitrary",)),
        name="moe_dispatch",
    )(zero_tiles, pos_tiles, h2)


def _moe_kernel(ie_ref, it0_ref, int_ref, inz_ref, xs_hbm, wgu_hbm, wd_hbm, bgu_ref, bd_ref,
                ys_hbm, stage, xb, act, out, wf, wb, sem_x, sem_w, sem_out, *, layer, n1, n2, f):
    s = pl.program_id(0)
    n_items = pl.num_programs(0)
    nch = n1 + n2
    nt = int_ref[s]
    nz = inz_ref[s]
    t0 = it0_ref[s]
    e = ie_ref[s]
    sn = jnp.minimum(s + 1, n_items - 1)
    has_next = (s + 1 < n_items) & (int_ref[sn] > 0)
    e_next = ie_ref[sn]
    t0_next = it0_ref[sn]
    nt_next = int_ref[sn]
    cast_rows = wf.shape[1] // MOE_CAST_SLICES

    def w_start(ex, kk, slot):
        @pl.when(kk < n1)
        def _():
            col = pl.multiple_of(kk * MOE_TF, MOE_TF)
            pltpu.make_async_copy(wgu_hbm.at[layer, ex, :, pl.ds(col, MOE_TF)],
                                  wf.at[slot, :, pl.ds(0, MOE_TF)], sem_w.at[slot]).start()
            pltpu.make_async_copy(wgu_hbm.at[layer, ex, :, pl.ds(f + col, MOE_TF)],
                                  wf.at[slot, :, pl.ds(MOE_TF, MOE_TF)], sem_w.at[slot]).start()

        @pl.when(kk >= n1)
        def _():
            col = pl.multiple_of((kk - n1) * MOE_TN, MOE_TN)
            pltpu.make_async_copy(wd_hbm.at[layer, ex, :, pl.ds(col, MOE_TN)], wf.at[slot], sem_w.at[slot]).start()

    def w_wait(slot):
        pltpu.make_async_copy(wd_hbm.at[layer, 0, :, pl.ds(0, MOE_TN)], wf.at[slot], sem_w.at[slot]).wait()

    def cast_slice(slot, q):
        r0 = q * cast_rows if isinstance(q, int) else pl.multiple_of(q * cast_rows, cast_rows)
        wb[slot, pl.ds(r0, cast_rows), :] = wf[slot, pl.ds(r0, cast_rows), :].astype(BF16)

    def x_copy(tile, slot):
        return pltpu.make_async_copy(xs_hbm.at[pl.ds(tile * MOE_SUB, MOE_SUB), :], stage.at[slot], sem_x.at[slot])

    def x_land(slot, j):
        r0 = pl.multiple_of(j * MOE_SUB, MOE_SUB)
        xb[pl.ds(r0, MOE_SUB), :] = stage[slot].astype(BF16)

    def copy_out(n, j):
        col = n * MOE_TN if isinstance(n, int) else pl.multiple_of(n * MOE_TN, MOE_TN)
        return pltpu.make_async_copy(
            out.at[n, pl.ds(j * MOE_SUB, MOE_SUB), :],
            ys_hbm.at[pl.ds((t0 + j) * MOE_SUB, MOE_SUB), pl.ds(col, MOE_TN)],
            sem_out.at[n])

    def sub_tiles(fn, other):
        def trip(p, carry):
            for u in range(MOE_GROUP):
                fn(MOE_GROUP * p + u)
            cast_slice(other, p)
            return carry

        full = nt // MOE_GROUP
        lax.fori_loop(0, full, trip, 0)
        rem = nt - full * MOE_GROUP
        for r in range(1, MOE_GROUP):
            @pl.when(rem == r)
            def _():
                for u in range(r):
                    fn(full * MOE_GROUP + u)
                cast_slice(other, full)

        trips = full + (rem > 0).astype(jnp.int32)
        for q in range(MOE_CAST_SLICES):
            @pl.when(q >= trips)
            def _():
                cast_slice(other, q)

    def advance(k):
        slot = k % 2
        k2 = k + 2

        @pl.when(k2 < nch)
        def _():
            w_start(e, k2, slot)

        @pl.when((k2 >= nch) & has_next)
        def _():
            w_start(e_next, k2 - nch, slot)

        @pl.when((k + 1 < nch) | has_next)
        def _():
            w_wait(1 - slot)
        return slot

    @pl.when(nz > 0)
    def _():
        out[0, pl.ds(0, MOE_SUB), :] = jnp.zeros((MOE_SUB, MOE_TN), out.dtype)

        def copy_zero(n, j):
            return pltpu.make_async_copy(
                out.at[0, pl.ds(0, MOE_SUB), :],
                ys_hbm.at[pl.ds((t0 + j) * MOE_SUB, MOE_SUB), pl.ds(n * MOE_TN, MOE_TN)], sem_out.at[n])

        for j in range(MOE_ITEM_SUBS):
            @pl.when(j < nz)
            def _():
                for n in range(n2):
                    copy_zero(n, j).start()
        for j in range(MOE_ITEM_SUBS):
            @pl.when(j < nz)
            def _():
                for n in range(n2):
                    copy_zero(n, j).wait()

    @pl.when((s == 0) & (nt > 0))
    def _():
        w_start(e, 0, 0)
        w_start(e, 1, 1)
        x_copy(t0, 0).start()
        for j in range(MOE_ITEM_SUBS):
            @pl.when(j < nt)
            def _():
                if j + 1 < MOE_ITEM_SUBS:
                    @pl.when(j + 1 < nt)
                    def _():
                        x_copy(t0 + j + 1, (j + 1) % 2).start()
                x_copy(t0 + j, j % 2).wait()
                x_land(j % 2, j)
        w_wait(0)
        for q in range(MOE_CAST_SLICES):
            cast_slice(0, q)

    @pl.when(nt > 0)
    def _():
        def up_step(k, carry):
            slot = advance(k)
            bg = bgu_ref[k]
            bl = bgu_ref[n1 + k]

            def up(j):
                r0 = pl.multiple_of(j * MOE_SUB, MOE_SUB)
                x = xb[pl.ds(r0, MOE_SUB), :]
                glu = jnp.minimum(_dot(x, wb[slot, :, pl.ds(0, MOE_TF)]) + bg, SWIGLU_LIMIT)
                lin = jnp.clip(_dot(x, wb[slot, :, pl.ds(MOE_TF, MOE_TF)]) + bl, -SWIGLU_LIMIT, SWIGLU_LIMIT)
                act[k, pl.ds(r0, MOE_SUB), :] = (glu * jax.nn.sigmoid(SWIGLU_ALPHA * glu) * (lin + 1.0)).astype(BF16)

            sub_tiles(up, 1 - slot)
            return carry

        lax.fori_loop(0, n1, up_step, 0)

        def down_step(n, carry):
            slot = advance(n1 + n)
            bd = bd_ref[n]
            for jj in range(2):
                @pl.when(has_next & (2 * n + jj < nt_next))
                def _():
                    x_copy(t0_next + 2 * n + jj, jj).start()

            def down(j):
                r0 = pl.multiple_of(j * MOE_SUB, MOE_SUB)
                a = jnp.concatenate([act[k, pl.ds(r0, MOE_SUB), :] for k in range(n1)], axis=1)
                out[n, pl.ds(r0, MOE_SUB), :] = _dot(a, wb[slot]) + bd

            sub_tiles(down, 1 - slot)
            for j in range(MOE_ITEM_SUBS):
                @pl.when(j < nt)
                def _():
                    copy_out(n, j).start()
            for jj in range(2):
                @pl.when(has_next & (2 * n + jj < nt_next))
                def _():
                    x_copy(t0_next + 2 * n + jj, jj).wait()
                    x_land(jj, 2 * n + jj)
            return carry

        lax.fori_loop(0, n2, down_step, 0)
        for n in range(n2):
            for j in range(MOE_ITEM_SUBS):
                @pl.when(j < nt)
                def _():
                    copy_out(n, j).wait()


def _moe_experts(xs, items, w_gu, b_gu, w_down, b_down, layer):
    rp, d = xs.shape
    n_layers, n_exp, _, f2 = w_gu.shape
    f = f2 // 2
    assert f == d and MOE_TN == 2 * MOE_TF and MOE_ITEM_SUBS <= 2 * (d // MOE_TN)
    assert d % MOE_CAST_SLICES == 0 and -(-MOE_ITEM_SUBS // MOE_GROUP) <= MOE_CAST_SLICES
    n1 = f // MOE_TF
    n2 = d // MOE_TN
    item_e, item_t0, item_nt, item_nz = items
    n_items = item_e.shape[0]
    tm = MOE_ITEM_SUBS * MOE_SUB
    kern = functools.partial(_moe_kernel, layer=layer, n1=n1, n2=n2, f=f)
    return pl.pallas_call(
        kern,
        grid_spec=pltpu.PrefetchScalarGridSpec(
            num_scalar_prefetch=4,
            grid=(n_items,),
            in_specs=[
                pl.BlockSpec(memory_space=pl.ANY),
                pl.BlockSpec(memory_space=pl.ANY),
                pl.BlockSpec(memory_space=pl.ANY),
                pl.BlockSpec((None, None, 2 * n1, 1, MOE_TF), lambda s, ie, it0, nt, nz: (layer, ie[s], 0, 0, 0)),
                pl.BlockSpec((None, None, n2, 1, MOE_TN), lambda s, ie, it0, nt, nz: (layer, ie[s], 0, 0, 0)),
            ],
            out_specs=pl.BlockSpec(memory_space=pl.ANY),
            scratch_shapes=[
                pltpu.VMEM((2, MOE_SUB, d), F32),
                pltpu.VMEM((tm, d), BF16),
                pltpu.VMEM((n1, tm, MOE_TF), BF16),
                pltpu.VMEM((n2, tm, MOE_TN), F32),
                pltpu.VMEM((2, d, MOE_TN), F32),
                pltpu.VMEM((2, d, MOE_TN), BF16),
                pltpu.SemaphoreType.DMA((2,)),
                pltpu.SemaphoreType.DMA((2,)),
                pltpu.SemaphoreType.DMA((n2,)),
            ],
        ),
        out_shape=jax.ShapeDtypeStruct((rp, d), F32),
        compiler_params=_cparams(("arbitrary",)),
        name="moe_experts",
    )(item_e, item_t0, item_nt, item_nz, xs, w_gu, w_down,
      b_gu.reshape(n_layers, n_exp, 2 * n1, 1, MOE_TF), b_down.reshape(n_layers, n_exp, n2, 1, MOE_TN))


def _moe_plan(meta, counts, n_experts, rows_padded):
    n = meta.shape[0]
    top_i = meta[:, :TOP_K]
    rank = meta[:, TOP_K:2 * TOP_K]
    counts = counts[0, :n_experts].astype(jnp.int32)
    ntiles = (counts + MOE_SUB - 1) // MOE_SUB
    tile_end = jnp.cumsum(ntiles)
    tile_start = tile_end - ntiles
    experts = jnp.arange(n_experts, dtype=jnp.int32)
    start_sel = jnp.sum(jnp.where(top_i[:, :, None] == experts[None, None, :], tile_start[None, None, :], 0), axis=-1)
    pos = start_sel * MOE_SUB + rank
    n_items_max = n_experts + (n * TOP_K // MOE_SUB + MOE_ITEM_SUBS - 1) // MOE_ITEM_SUBS
    items_e = (ntiles + MOE_ITEM_SUBS - 1) // MOE_ITEM_SUBS
    item_end = jnp.cumsum(items_e)
    item_start = item_end - items_e
    sidx = jnp.arange(n_items_max, dtype=jnp.int32)
    valid = sidx < item_end[-1]
    ie = jnp.sum((sidx[:, None] >= item_end[None, :]).astype(jnp.int32), axis=1)
    ie = jnp.minimum(ie, n_experts - 1)
    pick = ie[:, None] == experts[None, :]
    local = sidx - jnp.sum(jnp.where(pick, item_start[None, :], 0), axis=1)
    t0 = jnp.sum(jnp.where(pick, tile_start[None, :], 0), axis=1) + local * MOE_ITEM_SUBS
    nt = jnp.clip(jnp.sum(jnp.where(pick, ntiles[None, :], 0), axis=1) - local * MOE_ITEM_SUBS, 0, MOE_ITEM_SUBS)
    last_e = jnp.max(jnp.where(valid, ie, 0))
    ie = jnp.where(valid, ie, last_e).astype(jnp.int32)
    z0 = tile_end[-1] + (sidx - item_end[-1]) * MOE_ITEM_SUBS
    nz = jnp.clip(rows_padded // MOE_SUB - z0, 0, MOE_ITEM_SUBS)
    t0 = jnp.where(valid, t0, z0).astype(jnp.int32)
    nt = jnp.where(valid, nt, 0).astype(jnp.int32)
    nz = jnp.where(valid, 0, nz).astype(jnp.int32)
    total_tiles = rows_padded // MOE_SUB
    tail = tile_end[-1] + jnp.arange(total_tiles - n * TOP_K // MOE_SUB, dtype=jnp.int32)
    zero_tiles = jnp.concatenate([jnp.where(ntiles > 0, tile_end - 1, -1),
                                  jnp.where(tail < total_tiles, tail, -1)]).astype(jnp.int32)
    return pos.astype(jnp.int32), zero_tiles, (ie, t0, nt, nz)


def _combine_kernel(tc_ref, pos_ref, posn_ref, x_ref, w_ref, gf_ref, ys_hbm, *rest, split_tiles):
    o_refs, (buf, sem) = rest[:-2], rest[-2:]
    i = pl.program_id(0)
    n_tiles = pl.num_programs(0)
    rows = x_ref.shape[0]

    def issue(pref, slot):
        def body(r, carry):
            for k in range(TOP_K):
                p = pref[0, r * TOP_K + k]
                pltpu.make_async_copy(ys_hbm.at[pl.ds(p, 1), :], buf.at[slot, k, pl.ds(r, 1), :],
                                      sem.at[slot]).start()
            return carry

        lax.fori_loop(0, rows, body, 0, unroll=4)

    slot = i % 2

    @pl.when(i == 0)
    def _():
        issue(pos_ref, 0)

    @pl.when(i + 1 < n_tiles)
    def _():
        issue(posn_ref, 1 - slot)

    for k in range(TOP_K):
        pltpu.make_async_copy(ys_hbm.at[pl.ds(0, rows), :], buf.at[slot, k], sem.at[slot]).wait()
    w = w_ref[...]
    acc = w[:, 0:1] * buf[slot, 0]
    for k in range(1, TOP_K):
        acc = acc + w[:, k:k + 1] * buf[slot, k]
    val = x_ref[...] + gf_ref[...] * acc
    if split_tiles is None:
        o_refs[0][...] = val
    else:
        @pl.when(i < split_tiles)
        def _():
            o_refs[0][...] = val

        @pl.when(i >= split_tiles)
        def _():
            o_refs[1][...] = val


def _combine(x, ys, pos_tiles, wts, tile_cond, gf, split_rows=None):
    n, d = x.shape
    tm = TOKEN_TILE
    n_tiles = n // tm
    pos_spec = lambda f: pl.BlockSpec((None, 1, tm * TOP_K), f, memory_space=pltpu.SMEM)
    if split_rows is None:
        split_tiles = None
        out_specs = pl.BlockSpec((tm, d), lambda i, tc: (i, 0))
        out_shape = jax.ShapeDtypeStruct((n, d), F32)
    else:
        split_tiles = split_rows // tm
        out_specs = [pl.BlockSpec((tm, d), lambda i, tc: (jnp.minimum(i, split_tiles - 1), 0)),
                     pl.BlockSpec((tm, d), lambda i, tc: (jnp.maximum(i - split_tiles, 0), 0))]
        out_shape = [jax.ShapeDtypeStruct((split_rows, d), F32), jax.ShapeDtypeStruct((n - split_rows, d), F32)]
    return pl.pallas_call(
        functools.partial(_combine_kernel, split_tiles=split_tiles),
        grid_spec=pltpu.PrefetchScalarGridSpec(
            num_scalar_prefetch=1,
            grid=(n_tiles,),
            in_specs=[
                pos_spec(lambda i, tc: (i, 0, 0)),
                pos_spec(lambda i, tc: (jnp.minimum(i + 1, n_tiles - 1), 0, 0)),
                pl.BlockSpec((tm, d), lambda i, tc: (i, 0)),
                pl.BlockSpec((tm, LANES), lambda i, tc: (i, 0)),
                pl.BlockSpec((None, 1, d), lambda i, tc: (tc[i], 0, 0)),
                pl.BlockSpec(memory_space=pl.ANY),
            ],
            out_specs=out_specs,
            scratch_shapes=[
                pltpu.VMEM((2, TOP_K, tm, d), F32),
                pltpu.SemaphoreType.DMA((2,)),
            ],
        ),
        out_shape=out_shape,
        compiler_params=_cparams(("arbitrary",)),
        name="moe_combine",
    )(tile_cond, pos_tiles, pos_tiles, x, wts, gf, ys)


def _moe(x, h2, meta, wts, counts, tile_cond, gf, w_gu, b_gu, w_down, b_down, layer, split_rows=None):
    n, d = x.shape
    n_exp = w_gu.shape[1]
    rows_padded = n * TOP_K + n_exp * MOE_SUB
    pos, zero_tiles, items = _moe_plan(meta, counts, n_exp, rows_padded)
    pos_tiles = pos.reshape(n // TOKEN_TILE, 1, TOKEN_TILE * TOP_K)
    xs = _dispatch(h2, pos_tiles, zero_tiles, rows_padded)
    ys = _moe_experts(xs, items, w_gu, b_gu, w_down, b_down, layer)
    return _combine(x, ys, pos_tiles, wts, tile_cond, gf, split_rows)


def _tile_cond(ns, npr, seq, tm):
    return jnp.asarray(np.concatenate([1 + np.arange(ns // tm) // (seq // tm),
                                       np.zeros(npr // tm, np.int64)]).astype(np.int32))


def kernel(x_prompt, x_sample, cache_na_k, cache_na_v, state_lru, c, c_ctx, norm_mix, norm_ffn, w_ada, b_ada, ev_w_in, ev_w_out, na_q_norm, na_k_norm, na_rpb, lru_conv_w, lru_conv_b, lru_wa, lru_ba, lru_wx, lru_bx, lru_lambda, od_w_in, od_w_out, sgu_norm, sgu_w, sgu_b, router_w, router_b, moe_w_gu, moe_b_gu, moe_w_down, moe_b_down):
    bp, sp_len, d = x_prompt.shape
    bs, ss_len, _ = x_sample.shape
    depth = w_ada.shape[0]
    heads, dh = cache_na_k.shape[3], cache_na_k.shape[4]
    na_w = heads * dh
    lru_w = state_lru.shape[-1]
    n_exp = router_w.shape[-1]
    ns, npr = bs * ss_len, bp * sp_len
    assert bs == REC_SEQS and bp % REC_SEQS == 0
    assert ss_len % INPROJ_TILE == 0 and npr % INPROJ_TILE == 0 and sp_len % TOKEN_TILE == 0

    x_parts = [x_sample.reshape(ns, d), x_prompt.reshape(npr, d)]
    tc_in = _tile_cond(ns, npr, ss_len, INPROJ_TILE)
    tc_proj = _tile_cond(ns, npr, ss_len, ROW_TILE)
    tc_tok = _tile_cond(ns, npr, ss_len, TOKEN_TILE)
    cond_rows = 16
    cond = jnp.concatenate([c_ctx[None, :], c, jnp.zeros((cond_rows - 1 - bs, d), F32)], axis=0)
    mod = _adaln(cond, w_ada, b_ada)
    mod = mod.reshape(depth, cond_rows, 6, d).transpose(0, 2, 1, 3).reshape(depth, 6, cond_rows, 1, d)

    new_k, new_v, new_h = [], [], []
    for layer in range(depth):
        sm, scm, gm, sf, scf, gf = [mod[layer, i] for i in range(6)]
        g_mix = norm_mix[layer][None, :]
        g_ffn = norm_ffn[layer][None, :]
        if layer % 2 == 0:
            e = layer // 2
            ev_in = ev_w_in.shape[-1]
            head_gain = jnp.concatenate([jnp.tile(na_q_norm[e], heads), jnp.tile(na_k_norm[e], heads),
                                         jnp.ones((ev_in - 2 * na_w,), F32)])[None, :]
            proj = _inproj(x_parts, tc_in, g_mix, scm, sm, ev_w_in[e].astype(BF16), head_gain,
                           mode="heads", n_norm_cols=2 * na_w)
            att_s = _na_attention(proj, cache_na_k[:, e].reshape(bs, -1, na_w),
                                  cache_na_v[:, e].reshape(bs, -1, na_w), _na_bias_table(na_rpb[e]),
                                  nseq=bs, seq=ss_len, heads=heads, dh=dh)
            att_p = _ctx_attention(proj, row0=ns, nseq=bp, seq=sp_len, heads=heads, dh=dh)
            wgate, bgate = _pack_gate_weights(lru_wa[e], lru_ba[e], lru_wx[e], lru_bx[e])
            rec_args = (lru_conv_w[e], lru_conv_b[e][None, :], wgate, bgate, lru_lambda[e])
            h0_s = state_lru[:, e].transpose(1, 0, 2)[None]
            rec_s, _ = _recurrent(proj, *rec_args, h0_s, row0=0, nseq=bs, seq=ss_len,
                                  xr_col=3 * na_w, yg_col=3 * na_w + lru_w, width=lru_w)
            h0_p = jnp.zeros((bp // REC_SEQS, 2, REC_SEQS, lru_w), F32)
            rec_p, hl = _recurrent(proj, *rec_args, h0_p, row0=ns, nseq=bp, seq=sp_len,
                                   xr_col=3 * na_w, yg_col=3 * na_w + lru_w, width=lru_w)
            new_k.append(proj[ns:, na_w:2 * na_w].reshape(bp, sp_len, heads, dh))
            new_v.append(proj[ns:, 2 * na_w:3 * na_w].reshape(bp, sp_len, heads, dh))
            new_h.append(hl.transpose(0, 2, 1, 3).reshape(bp, 2, lru_w))
            w_out = ev_w_out[e].astype(BF16)
            lhs_list, w_list = [[att_s, att_p], [rec_s, rec_p]], [w_out[:na_w], w_out[na_w:]]
        else:
            o = layer // 2
            z = _inproj(x_parts, tc_in, g_mix, scm, sm, od_w_in[o].astype(BF16),
                        jnp.ones((1, od_w_in.shape[-1]), F32), mode="gelu", n_norm_cols=0)
            chunk = sgu_w.shape[-1]
            sg = _sgu(z, sgu_norm[o][None, :], sgu_w[o].astype(BF16), sgu_b[o].T, chunk=chunk)
            lhs_list, w_list = [[sg]], [od_w_out[o].astype(BF16)]
        rw = jnp.pad(router_w[layer], ((0, 0), (0, LANES - n_exp)))
        rw_hi, rw_lo = _split_bf16(rw)
        rb = jnp.pad(router_b[layer], (0, LANES - n_exp))[None, :]
        x, h2, meta, wts, counts = _outproj(lhs_list, w_list, x_parts, tc_proj, gm, g_ffn, scf, sf,
                                             rw_hi, rw_lo, rb, n_experts=n_exp)
        x = _moe(x, h2, meta, wts, counts, tc_tok, gf, moe_w_gu, moe_b_gu, moe_w_down, moe_b_down, layer,
                 split_rows=ns if layer == depth - 1 else None)
        x_parts = list(x) if layer == depth - 1 else [x]

    y_sample = x_parts[0].reshape(bs, ss_len, d)
    y_prompt = x_parts[1].reshape(bp, sp_len, d)
    new_na_k = jnp.stack(new_k, axis=1)
    new_na_v = jnp.stack(new_v, axis=1)
    new_state_lru = jnp.stack(new_h, axis=1)
    return (y_prompt, y_sample, new_na_k, new_na_v, new_state_lru)
```

```python
import functools

import numpy as np
import jax
import jax.numpy as jnp
from jax import lax
from jax.experimental import pallas as pl
from jax.experimental.pallas import tpu as pltpu

F32 = jnp.float32
BF16 = jnp.bfloat16

EPS = 1e-6
NEG_INF = -1e30
GRID_W = 64
NA_KH = 8
NA_KW = 16
LRU_C = 8.0
SWIGLU_ALPHA = 1.702
SWIGLU_LIMIT = 7.0
TOP_K = 4

LANES = 128
INPROJ_TILE = 1024
ROW_TILE = 512
PROJ_CHUNK = 256
TOKEN_TILE = 256
MOE_SUB = 256
MOE_ITEM_SUBS = 7
MOE_TF = 256
MOE_TN = 512
MOE_GROUP = 3
MOE_CAST_SLICES = 4
VMEM_LIMIT = 56 * 1024 * 1024


def _cparams(sem, vmem=VMEM_LIMIT):
    return pltpu.CompilerParams(dimension_semantics=sem, vmem_limit_bytes=vmem)


def _split_bf16(x):
    hi = x.astype(BF16)
    lo = (x - hi.astype(F32)).astype(BF16)
    return hi, lo


def _gelu_tanh(x):
    return 0.5 * x * (1.0 + jnp.tanh(0.7978845608028654 * (x + 0.044715 * x * x * x)))


def _dot(a, b):
    return jnp.dot(a, b, preferred_element_type=F32)


def _dot_nt(a, b):
    return lax.dot_general(a, b, (((1,), (1,)), ((), ())), preferred_element_type=F32)


def _adaln_kernel(c_ref, w_ref, b_ref, o_ref):
    cnd = c_ref[...]
    s = cnd * jax.nn.sigmoid(cnd)
    hi, lo = _split_bf16(s)
    w = w_ref[...].astype(BF16)
    o_ref[...] = _dot(hi, w) + _dot(lo, w) + b_ref[...]


def _adaln(cond, w_ada, b_ada, tn=1024):
    depth, d, n6 = w_ada.shape
    rows = cond.shape[0]
    return pl.pallas_call(
        _adaln_kernel,
        grid=(depth, n6 // tn),
        in_specs=[
            pl.BlockSpec((rows, d), lambda l, j: (0, 0)),
            pl.BlockSpec((None, d, tn), lambda l, j: (l, 0, j)),
            pl.BlockSpec((None, 1, tn), lambda l, j: (l, 0, j)),
        ],
        out_specs=pl.BlockSpec((None, rows, tn), lambda l, j: (l, 0, j)),
        out_shape=jax.ShapeDtypeStruct((depth, rows, n6), F32),
        compiler_params=_cparams(("arbitrary", "arbitrary")),
        name="adaln",
    )(cond, w_ada, b_ada.reshape(depth, 1, n6))


def _modulated_norm(x, g, sc, sh):
    ms = jnp.mean(x * x, axis=-1, keepdims=True)
    return (x * lax.rsqrt(ms + EPS) * g) * (1.0 + sc) + sh


def _row_specs(parts, tm, width, **spec_kwargs):
    specs, bounds, off = [], [], 0
    for a in parts:
        nt = a.shape[0] // tm
        specs.append(pl.BlockSpec((tm, width), lambda i, *_, off=off, nt=nt: (jnp.clip(i - off, 0, nt - 1), 0),
                                  **spec_kwargs))
        off += nt
        bounds.append(off)
    return specs, tuple(bounds)


def _row_select(refs, bounds, i, idx):
    val = refs[-1][idx]
    for p in range(len(refs) - 2, -1, -1):
        val = jnp.where(i < bounds[p], refs[p][idx], val)
    return val


def _inproj_kernel(tc_ref, *refs, mode, n_norm_tiles, bounds):
    x_refs = refs[:len(bounds)]
    g_ref, sc_ref, sh_ref, w_ref, hg_ref, o_ref, h_scr = refs[len(bounds):]
    i = pl.program_id(0)
    j = pl.program_id(1)

    @pl.when(j == 0)
    def _():
        for rc in range(h_scr.shape[0] // PROJ_CHUNK):
            rows = slice(rc * PROJ_CHUNK, (rc + 1) * PROJ_CHUNK)
            x = _row_select(x_refs, bounds, i, (rows, slice(None)))
            h_scr[rows, :] = _modulated_norm(x, g_ref[...], sc_ref[...], sh_ref[...]).astype(BF16)

    def chunks(epilogue):
        for c in range(o_ref.shape[1] // PROJ_CHUNK):
            cs = slice(c * PROJ_CHUNK, (c + 1) * PROJ_CHUNK)
            o_ref[:, cs] = epilogue(_dot(h_scr[...], w_ref[:, cs]), cs)

    def head_norm(acc, cs):
        hg = hg_ref[:, cs]
        parts = []
        for c in range(PROJ_CHUNK // LANES):
            blk = acc[:, c * LANES:(c + 1) * LANES]
            ms = jnp.mean(blk * blk, axis=-1, keepdims=True)
            parts.append(blk * lax.rsqrt(ms + EPS) * hg[:, c * LANES:(c + 1) * LANES])
        return jnp.concatenate(parts, axis=1)

    if mode == "gelu":
        chunks(lambda acc, cs: _gelu_tanh(acc))
    else:
        @pl.when(j < n_norm_tiles)
        def _():
            chunks(head_norm)

        @pl.when(j >= n_norm_tiles)
        def _():
            chunks(lambda acc, cs: acc)


def _inproj(x_parts, tile_cond, norm_g, sc, sh, w_bf16, head_gain, *, mode, n_norm_cols, tn=1024):
    n = sum(a.shape[0] for a in x_parts)
    d = x_parts[0].shape[1]
    nout = w_bf16.shape[1]
    tm = INPROJ_TILE
    x_specs, bounds = _row_specs(x_parts, tm, d, pipeline_mode=pl.Buffered(1))
    kern = functools.partial(_inproj_kernel, mode=mode, n_norm_tiles=n_norm_cols // tn, bounds=bounds)
    modspec = pl.BlockSpec((None, 1, d), lambda i, j, tc: (tc[i], 0, 0))
    return pl.pallas_call(
        kern,
        grid_spec=pltpu.PrefetchScalarGridSpec(
            num_scalar_prefetch=1,
            grid=(n // tm, nout // tn),
            in_specs=x_specs + [
                pl.BlockSpec((1, d), lambda i, j, tc: (0, 0)),
                modspec, modspec,
                pl.BlockSpec((d, tn), lambda i, j, tc: (0, j)),
                pl.BlockSpec((1, tn), lambda i, j, tc: (0, j)),
            ],
            out_specs=pl.BlockSpec((tm, tn), lambda i, j, tc: (i, j)),
            scratch_shapes=[pltpu.VMEM((tm, d), BF16)],
        ),
        out_shape=jax.ShapeDtypeStruct((n, nout), F32),
        compiler_params=_cparams(("arbitrary", "arbitrary")),
        name="inproj_" + mode,
    )(tile_cond, *x_parts, norm_g, sc, sh, w_bf16, head_gain)


def _ctx_attn_kernel(q_ref, k_ref, v_ref, o_ref, *, heads, dh):
    scale = dh ** -0.5
    for h in range(heads):
        sl = slice(h * dh, (h + 1) * dh)
        q = q_ref[:, sl].astype(BF16)
        k = k_ref[:, sl].astype(BF16)
        v = v_ref[:, sl].astype(BF16)
        s = _dot_nt(q, k) * scale
        m = jnp.max(s, axis=-1, keepdims=True)
        p = jnp.exp(s - m)
        den = jnp.sum(p, axis=-1, keepdims=True)
        o = _dot(p.astype(BF16), v) / den
        o_ref[:, sl] = o.astype(o_ref.dtype)


def _ctx_attention(proj, *, row0, nseq, seq, heads, dh):
    width = heads * dh
    rb0 = row0 // seq
    kern = functools.partial(_ctx_attn_kernel, heads=heads, dh=dh)
    return pl.pallas_call(
        kern,
        grid=(nseq,),
        in_specs=[
            pl.BlockSpec((seq, width), lambda b: (rb0 + b, 0)),
            pl.BlockSpec((seq, width), lambda b: (rb0 + b, 1)),
            pl.BlockSpec((seq, width), lambda b: (rb0 + b, 2)),
        ],
        out_specs=pl.BlockSpec((seq, width), lambda b: (b, 0)),
        out_shape=jax.ShapeDtypeStruct((nseq * seq, width), BF16),
        compiler_params=_cparams(("arbitrary",)),
        name="ctx_attention",
    )(proj, proj, proj)


def _na_attn_kernel(q_ref, k_ref, v_ref, kc_ref, vc_ref, bias_ref, o_ref, s_scr, p_scr, *, rows, dh):
    scale = dh ** -0.5
    win = NA_KH * GRID_W
    qc = lax.broadcasted_iota(jnp.int32, (GRID_W, win), 0)
    kc = lax.broadcasted_iota(jnp.int32, (GRID_W, win), 1) % GRID_W
    c_start = jnp.clip(qc - NA_KW // 2, 0, GRID_W - NA_KW)
    col_ok = (kc >= c_start) & (kc < c_start + NA_KW)
    v_ctx = vc_ref[...].astype(BF16)
    s_scr[:, win:] = _dot_nt(q_ref[...].astype(BF16), kc_ref[...].astype(BF16)) * scale
    for r in range(rows):
        rs = min(max(r - NA_KH // 2, 0), rows - NA_KH)
        q = q_ref[r * GRID_W:(r + 1) * GRID_W, :].astype(BF16)
        kw = k_ref[rs * GRID_W:rs * GRID_W + win, :].astype(BF16)
        s_win = _dot_nt(q, kw) * scale + bias_ref[r - rs]
        s_scr[r * GRID_W:(r + 1) * GRID_W, :win] = jnp.where(col_ok, s_win, NEG_INF)
    blk = 2 * GRID_W
    inv = []
    for c in range(s_scr.shape[0] // blk):
        s = s_scr[c * blk:(c + 1) * blk, :]
        p = jnp.exp(s - jnp.max(s, axis=-1, keepdims=True))
        inv.append(1.0 / jnp.sum(p, axis=-1, keepdims=True))
        p_scr[c * blk:(c + 1) * blk, :] = p.astype(BF16)
    o_ctx = _dot(p_scr[:, win:], v_ctx)
    for r in range(rows):
        rs = min(max(r - NA_KH // 2, 0), rows - NA_KH)
        rr = slice(r * GRID_W, (r + 1) * GRID_W)
        vw = v_ref[rs * GRID_W:rs * GRID_W + win, :].astype(BF16)
        o = _dot(p_scr[rr, :win], vw) + o_ctx[rr, :]
        half = inv[r // 2][(r % 2) * GRID_W:(r % 2 + 1) * GRID_W, :]
        o_ref[rr, :] = (o * half).astype(o_ref.dtype)


def _na_bias_kernel(rpb_ref, o_ref, *, n_rows, n_cols):
    h = pl.program_id(0)
    q = lax.broadcasted_iota(jnp.int32, (GRID_W, GRID_W), 0)
    k = lax.broadcasted_iota(jnp.int32, (GRID_W, GRID_W), 1)
    col_idx = jnp.clip(k - q + NA_KW - 1, 0, n_cols - 1)
    for ri in range(n_rows):
        acc = jnp.zeros((GRID_W, GRID_W), F32)
        for j in range(n_cols):
            acc = jnp.where(col_idx == j, rpb_ref[(h * n_rows + ri) * n_cols + j], acc)
        o_ref[ri] = acc


def _na_bias_table(rpb):
    heads, n_rows, n_cols = rpb.shape
    toep = pl.pallas_call(
        functools.partial(_na_bias_kernel, n_rows=n_rows, n_cols=n_cols),
        grid=(heads,),
        in_specs=[pl.BlockSpec(memory_space=pltpu.SMEM)],
        out_specs=pl.BlockSpec((None, n_rows, GRID_W, GRID_W), lambda h: (h, 0, 0, 0)),
        out_shape=jax.ShapeDtypeStruct((heads, n_rows, GRID_W, GRID_W), F32),
        compiler_params=_cparams(("arbitrary",)),
        name="na_bias",
    )(rpb.reshape(-1))
    classes = []
    for d in range(NA_KH):
        t = toep[:, NA_KH - 1 - d:2 * NA_KH - 1 - d]
        classes.append(t.transpose(0, 2, 1, 3).reshape(heads, GRID_W, NA_KH * GRID_W))
    return jnp.stack(classes, axis=1)


def _na_attention(proj, cache_k, cache_v, bias_tab, *, nseq, seq, heads, dh):
    past = cache_k.shape[1]
    rows = seq // GRID_W
    kern = functools.partial(_na_attn_kernel, rows=rows, dh=dh)
    return pl.pallas_call(
        kern,
        grid=(heads, nseq),
        in_specs=[
            pl.BlockSpec((seq, dh), lambda h, b: (b, h)),
            pl.BlockSpec((seq, dh), lambda h, b: (b, heads + h)),
            pl.BlockSpec((seq, dh), lambda h, b: (b, 2 * heads + h)),
            pl.BlockSpec((None, past, dh), lambda h, b: (b, 0, h)),
            pl.BlockSpec((None, past, dh), lambda h, b: (b, 0, h)),
            pl.BlockSpec((None, NA_KH, GRID_W, NA_KH * GRID_W), lambda h, b: (h, 0, 0, 0)),
        ],
        out_specs=pl.BlockSpec((seq, dh), lambda h, b: (b, h)),
        out_shape=jax.ShapeDtypeStruct((nseq * seq, heads * dh), BF16),
        scratch_shapes=[
            pltpu.VMEM((seq, NA_KH * GRID_W + past), F32),
            pltpu.VMEM((seq, NA_KH * GRID_W + past), BF16),
        ],
        compiler_params=_cparams(("arbitrary", "arbitrary")),
        name="na_attention",
    )(proj, proj, proj, cache_k, cache_v, bias_tab)


REC_SEQS = 8
REC_CHUNK = 256


def _rec_kernel(xr_ref, yg_ref, cw_ref, cb_ref, wg_ref, bg_ref, lam_ref, h0_ref,
                out_ref, hl_ref, a_scr, b_scr, *, seq):
    pitch = seq + 8
    total = REC_SEQS * seq
    nchunks = total // REC_CHUNK
    row = lax.broadcasted_iota(jnp.int32, (REC_CHUNK, LANES), 0)
    cw = cw_ref[...]
    cb = cb_ref[...]
    wg_hi = wg_ref[0]
    wg_lo = wg_ref[1]
    bg = bg_ref[...]
    lam = lam_ref[...]
    sp = jnp.maximum(-lam, 0.0) + jnp.log(1.0 + jnp.exp(-jnp.abs(lam)))

    def coeff_chunk(ci, carry):
        r0 = pl.multiple_of(ci * REC_CHUNK, REC_CHUNK)
        t0 = r0 % seq
        x = xr_ref[pl.ds(r0, REC_CHUNK), :]
        prev = xr_ref[pl.ds(pl.multiple_of(jnp.maximum(r0 - 8, 0), 8), 8), :]
        nxt = xr_ref[pl.ds(pl.multiple_of(jnp.minimum(r0 + REC_CHUNK, total - 8), 8), 8), :]
        has_prev = (t0 > 0).astype(F32)
        has_next = (t0 + REC_CHUNK < seq).astype(F32)
        p6 = prev[6:7, :] * has_prev
        p7 = prev[7:8, :] * has_prev
        n0 = nxt[0:1, :] * has_next
        x_m1 = jnp.where(row == 0, p7, pltpu.roll(x, 1, axis=0))
        x_m2 = jnp.where(row == 0, p6, jnp.where(row == 1, p7, pltpu.roll(x, 2, axis=0)))
        x_p1 = jnp.where(row == REC_CHUNK - 1, n0, pltpu.roll(x, REC_CHUNK - 1, axis=0))
        xc = cb + x_m2 * cw[0:1, :] + x_m1 * cw[1:2, :] + x * cw[2:3, :] + x_p1 * cw[3:4, :]
        xh, xl = _split_bf16(xc)
        gates = _dot(xh, wg_hi) + _dot(xl, wg_hi) + _dot(xh, wg_lo) + bg
        srow = pl.multiple_of((r0 // seq) * pitch + t0, 8)
        for d in range(2):
            r = jax.nn.sigmoid(gates[:, (2 * d) * LANES:(2 * d + 1) * LANES])
            i = jax.nn.sigmoid(gates[:, (2 * d + 1) * LANES:(2 * d + 2) * LANES])
            log_a = (-LRU_C) * r * sp[d:d + 1, :]
            a = jnp.exp(log_a)
            b = jnp.sqrt(1.0 - jnp.exp(2.0 * log_a)) * (i * xc)
            a_scr[d, pl.ds(srow, REC_CHUNK), :] = a
            b_scr[d, pl.ds(srow, REC_CHUNK), :] = b
        return carry

    lax.fori_loop(0, nchunks, coeff_chunk, 0)

    def scan_step(t, carry):
        hf, hb = carry
        tb = seq - 1 - t
        hf = a_scr[0, pl.ds(t, REC_SEQS, stride=pitch), :] * hf + b_scr[0, pl.ds(t, REC_SEQS, stride=pitch), :]
        b_scr[0, pl.ds(t, REC_SEQS, stride=pitch), :] = hf
        hb = a_scr[1, pl.ds(tb, REC_SEQS, stride=pitch), :] * hb + b_scr[1, pl.ds(tb, REC_SEQS, stride=pitch), :]
        b_scr[1, pl.ds(tb, REC_SEQS, stride=pitch), :] = hb
        return hf, hb

    hf, hb = lax.fori_loop(0, seq, scan_step, (h0_ref[0], h0_ref[1]), unroll=8)
    hl_ref[0] = hf
    hl_ref[1] = hb

    def out_chunk(ci, carry):
        r0 = pl.multiple_of(ci * REC_CHUNK, REC_CHUNK)
        srow = pl.multiple_of((r0 // seq) * pitch + r0 % seq, 8)
        hsum = b_scr[0, pl.ds(srow, REC_CHUNK), :] + b_scr[1, pl.ds(srow, REC_CHUNK), :]
        out_ref[pl.ds(r0, REC_CHUNK), :] = (_gelu_tanh(yg_ref[pl.ds(r0, REC_CHUNK), :]) * hsum).astype(out_ref.dtype)
        return carry

    lax.fori_loop(0, nchunks, out_chunk, 0)


def _recurrent(proj, conv_w, conv_b, wgate, bgate, lam, h0, *, row0, nseq, seq, xr_col, yg_col, width):
    groups = nseq // REC_SEQS
    blk_rows = REC_SEQS * seq
    rb0 = row0 // blk_rows
    nblk = width // LANES
    xc0 = xr_col // LANES
    yc0 = yg_col // LANES
    pitch = seq + 8
    kern = functools.partial(_rec_kernel, seq=seq)
    return pl.pallas_call(
        kern,
        grid=(groups, nblk),
        in_specs=[
            pl.BlockSpec((blk_rows, LANES), lambda g, n: (rb0 + g, xc0 + n)),
            pl.BlockSpec((blk_rows, LANES), lambda g, n: (rb0 + g, yc0 + n)),
            pl.BlockSpec((4, LANES), lambda g, n: (0, n)),
            pl.BlockSpec((1, LANES), lambda g, n: (0, n)),
            pl.BlockSpec((None, 2, LANES, 4 * LANES), lambda g, n: (n, 0, 0, 0)),
            pl.BlockSpec((None, 1, 4 * LANES), lambda g, n: (n, 0, 0)),
            pl.BlockSpec((2, LANES), lambda g, n: (0, n)),
            pl.BlockSpec((None, 2, REC_SEQS, LANES), lambda g, n: (g, 0, 0, n)),
        ],
        out_specs=[
            pl.BlockSpec((blk_rows, LANES), lambda g, n: (g, n)),
            pl.BlockSpec((None, 2, REC_SEQS, LANES), lambda g, n: (g, 0, 0, n)),
        ],
        out_shape=[
            jax.ShapeDtypeStruct((nseq * seq, width), BF16),
            jax.ShapeDtypeStruct((groups, 2, REC_SEQS, width), F32),
        ],
        scratch_shapes=[
            pltpu.VMEM((2, REC_SEQS * pitch, LANES), F32),
            pltpu.VMEM((2, REC_SEQS * pitch, LANES), F32),
        ],
        compiler_params=_cparams(("arbitrary", "arbitrary")),
        name="recurrent",
    )(proj, proj, conv_w, conv_b, wgate, bgate, lam, h0)


def _pack_gate_weights(wa, ba, wx, bx):
    nblk = wa.shape[1]
    w = jnp.concatenate([wa[0], wx[0], wa[1], wx[1]], axis=-1)
    hi, lo = _split_bf16(w)
    b = jnp.stack([ba[0].reshape(nblk, LANES), bx[0].reshape(nblk, LANES),
                   ba[1].reshape(nblk, LANES), bx[1].reshape(nblk, LANES)], axis=1)
    return jnp.stack([hi, lo], axis=1), b.reshape(nblk, 1, 4 * LANES)


def _sgu_kernel(u_ref, v_ref, g_ref, ws_ref, bs_ref, o_ref, *, chunk, groups):
    g = g_ref[...]
    for c in range(u_ref.shape[0] // chunk):
        rs = slice(c * chunk, (c + 1) * chunk)
        v = v_ref[rs, :]
        ms = jnp.mean(v * v, axis=-1, keepdims=True)
        vn = (v * lax.rsqrt(ms + EPS) * g).astype(BF16)
        for grp in range(groups):
            cs = slice(grp * LANES, (grp + 1) * LANES)
            mixed = _dot(ws_ref[grp], vn[:, cs]) + bs_ref[:, grp:grp + 1]
            o_ref[rs, cs] = (u_ref[rs, cs] * mixed).astype(o_ref.dtype)


def _sgu(z, v_norm, ws_bf16, bs_t, *, chunk, tm=512):
    n, w2 = z.shape
    w = w2 // 2
    groups = ws_bf16.shape[0]
    kern = functools.partial(_sgu_kernel, chunk=chunk, groups=groups)
    return pl.pallas_call(
        kern,
        grid=(n // tm,),
        in_specs=[
            pl.BlockSpec((tm, w), lambda i: (i, 0)),
            pl.BlockSpec((tm, w), lambda i: (i, 1)),
            pl.BlockSpec((1, w), lambda i: (0, 0)),
            pl.BlockSpec((groups, chunk, chunk), lambda i: (0, 0, 0)),
            pl.BlockSpec((chunk, groups), lambda i: (0, 0)),
        ],
        out_specs=pl.BlockSpec((tm, w), lambda i: (i, 0)),
        out_shape=jax.ShapeDtypeStruct((n, w), BF16),
        compiler_params=_cparams(("arbitrary",)),
        name="sgu",
    )(z, z, v_norm, ws_bf16, bs_t)


def _outproj_kernel(tc_ref, *refs, lhs_bounds, x_bounds, n_experts):
    tile = pl.program_id(0)
    refs = list(refs)
    lhs = [[refs.pop(0) for _ in b] for b in lhs_bounds]
    ws = [refs.pop(0) for _ in lhs_bounds]
    x_refs = [refs.pop(0) for _ in x_bounds]
    (gm_ref, g2_ref, scf_ref, sf_ref, rwh_ref, rwl_ref, rb_ref, tri_ref,
     xo_ref, h2_ref, meta_ref, wts_ref, cnt_ref, carry) = refs
    @pl.when(pl.program_id(0) == 0)
    def _():
        carry[...] = jnp.zeros(carry.shape, carry.dtype)

    count = carry[...]
    chunk = tri_ref.shape[0]
    for rc in range(xo_ref.shape[0] // chunk):
        rows = slice(rc * chunk, (rc + 1) * chunk)
        idx = (rows, slice(None))
        mix = _dot(_row_select(lhs[0], lhs_bounds[0], tile, idx), ws[0][...])
        for parts, bounds, w_ref in zip(lhs[1:], lhs_bounds[1:], ws[1:]):
            mix = mix + _dot(_row_select(parts, bounds, tile, idx), w_ref[...])
        xn = _row_select(x_refs, x_bounds, tile, idx) + gm_ref[...] * mix
        xo_ref[rows, :] = xn
        h2 = _modulated_norm(xn, g2_ref[...], scf_ref[...], sf_ref[...])
        hi, lo = _split_bf16(h2)
        h2_ref[rows, :] = h2
        logits = _dot(hi, rwh_ref[...]) + _dot(lo, rwh_ref[...]) + _dot(hi, rwl_ref[...]) + rb_ref[...]
        lane = lax.broadcasted_iota(jnp.int32, logits.shape, 1)
        l = jnp.where(lane < n_experts, logits, -jnp.inf)
        vals, idxs = [], []
        for _ in range(TOP_K):
            m = jnp.max(l, axis=-1, keepdims=True)
            i = jnp.min(jnp.where(l == m, lane, LANES), axis=-1, keepdims=True)
            vals.append(m)
            idxs.append(i)
            l = jnp.where(lane == i, -jnp.inf, l)
        es = [jnp.exp(v - vals[0]) for v in vals]
        den = es[0] + es[1] + es[2] + es[3]
        chosen = jnp.zeros(logits.shape, F32)
        for k in range(TOP_K):
            chosen = chosen + (lane == idxs[k]).astype(F32)
        before = count + _dot(tri_ref[...], chosen.astype(BF16))
        meta = jnp.zeros(logits.shape, jnp.int32)
        wts_out = jnp.zeros(logits.shape, F32)
        for k in range(TOP_K):
            rank = jnp.sum(jnp.where(lane == idxs[k], before, 0.0), axis=-1, keepdims=True).astype(jnp.int32)
            meta = jnp.where(lane == k, idxs[k], meta)
            meta = jnp.where(lane == TOP_K + k, rank, meta)
            wts_out = jnp.where(lane == k, es[k] / den, wts_out)
        meta_ref[rows, :] = meta
        wts_ref[rows, :] = wts_out
        count = count + jnp.sum(chosen, axis=0, keepdims=True)
    carry[...] = count
    cnt_ref[...] = count


def _outproj(lhs_list, w_list, x_parts, tile_cond, gm, g2, scf, sf, rw_hi, rw_lo, rb, *, n_experts):
    n = sum(a.shape[0] for a in x_parts)
    d = x_parts[0].shape[1]
    tm = ROW_TILE
    lhs_specs, lhs_bounds = [], []
    for parts in lhs_list:
        specs, bounds = _row_specs(parts, tm, parts[0].shape[1])
        lhs_specs += specs
        lhs_bounds.append(bounds)
    x_specs, x_bounds = _row_specs(x_parts, tm, d)
    kern = functools.partial(_outproj_kernel, lhs_bounds=tuple(lhs_bounds), x_bounds=x_bounds, n_experts=n_experts)
    modspec = pl.BlockSpec((None, 1, d), lambda i, tc: (tc[i], 0, 0))
    tri = jnp.asarray(np.tril(np.ones((tm, tm), np.float32), -1), BF16)
    in_specs = (
        lhs_specs
        + [pl.BlockSpec(w.shape, lambda i, tc: (0, 0)) for w in w_list]
        + x_specs
        + [modspec,
           pl.BlockSpec((1, d), lambda i, tc: (0, 0)),
           modspec, modspec,
           pl.BlockSpec((d, LANES), lambda i, tc: (0, 0)),
           pl.BlockSpec((d, LANES), lambda i, tc: (0, 0)),
           pl.BlockSpec((1, LANES), lambda i, tc: (0, 0)),
           pl.BlockSpec((tm, tm), lambda i, tc: (0, 0))])
    return pl.pallas_call(
        kern,
        grid_spec=pltpu.PrefetchScalarGridSpec(
            num_scalar_prefetch=1,
            grid=(n // tm,),
            in_specs=in_specs,
            out_specs=[
                pl.BlockSpec((tm, d), lambda i, tc: (i, 0)),
                pl.BlockSpec((tm, d), lambda i, tc: (i, 0)),
                pl.BlockSpec((tm, LANES), lambda i, tc: (i, 0)),
                pl.BlockSpec((tm, LANES), lambda i, tc: (i, 0)),
                pl.BlockSpec((1, LANES), lambda i, tc: (0, 0)),
            ],
            scratch_shapes=[pltpu.VMEM((1, LANES), F32)],
        ),
        out_shape=[
            jax.ShapeDtypeStruct((n, d), F32),
            jax.ShapeDtypeStruct((n, d), F32),
            jax.ShapeDtypeStruct((n, LANES), jnp.int32),
            jax.ShapeDtypeStruct((n, LANES), F32),
            jax.ShapeDtypeStruct((1, LANES), F32),
        ],
        compiler_params=_cparams(("arbitrary",)),
        name="outproj_router",
    )(tile_cond, *[a for parts in lhs_list for a in parts], *w_list, *x_parts,
      gm, g2, scf, sf, rw_hi, rw_lo, rb, tri)


def _dispatch_kernel(zt_ref, pos_ref, h_ref, xs_hbm, zbuf, sem, zsem):
    rows = h_ref.shape[0]

    @pl.when(pl.program_id(0) == 0)
    def _():
        zbuf[...] = jnp.zeros(zbuf.shape, zbuf.dtype)

        def zero_tile(i):
            return pltpu.make_async_copy(zbuf, xs_hbm.at[pl.ds(zt_ref[i] * MOE_SUB, MOE_SUB), :], zsem)

        for i in range(zt_ref.shape[0]):
            @pl.when(zt_ref[i] >= 0)
            def _():
                zero_tile(i).start()
        for i in range(zt_ref.shape[0]):
            @pl.when(zt_ref[i] >= 0)
            def _():
                zero_tile(i).wait()

    def issue(r, carry):
        for k in range(TOP_K):
            p = pos_ref[0, r * TOP_K + k]
            pltpu.make_async_copy(h_ref.at[pl.ds(r, 1), :], xs_hbm.at[pl.ds(p, 1), :], sem).start()
        return carry

    lax.fori_loop(0, rows, issue, 0, unroll=4)
    for k in range(TOP_K):
        pltpu.make_async_copy(h_ref, xs_hbm.at[pl.ds(0, rows), :], sem).wait()


def _dispatch(h2, pos_tiles, zero_tiles, rows_padded):
    n, d = h2.shape
    tm = TOKEN_TILE
    return pl.pallas_call(
        _dispatch_kernel,
        grid_spec=pltpu.PrefetchScalarGridSpec(
            num_scalar_prefetch=1,
            grid=(n // tm,),
            in_specs=[
                pl.BlockSpec((None, 1, tm * TOP_K), lambda i, zt: (i, 0, 0), memory_space=pltpu.SMEM),
                pl.BlockSpec((tm, d), lambda i, zt: (i, 0)),
            ],
            out_specs=pl.BlockSpec(memory_space=pl.ANY),
            scratch_shapes=[
                pltpu.VMEM((MOE_SUB, d), h2.dtype),
                pltpu.SemaphoreType.DMA,
                pltpu.SemaphoreType.DMA,
            ],
        ),
        out_shape=jax.ShapeDtypeStruct((rows_padded, d), h2.dtype),
        compiler_params=_cparams(("arbitrary",)),
        name="moe_dispatch",
    )(zero_tiles, pos_tiles, h2)


def _moe_kernel(ie_ref, it0_ref, int_ref, inz_ref, xs_hbm, wgu_hbm, wd_hbm, bgu_ref, bd_ref,
                ys_hbm, stage, xb, act, out, wf, wb, sem_x, sem_w, sem_out, *, layer, n1, n2, f):
    s = pl.program_id(0)
    n_items = pl.num_programs(0)
    nch = n1 + n2
    nt = int_ref[s]
    nz = inz_ref[s]
    t0 = it0_ref[s]
    e = ie_ref[s]
    sn = jnp.minimum(s + 1, n_items - 1)
    has_next = (s + 1 < n_items) & (int_ref[sn] > 0)
    e_next = ie_ref[sn]
    t0_next = it0_ref[sn]
    nt_next = int_ref[sn]
    nt_prev = int_ref[jnp.maximum(s - 1, 0)]
    has_prev = (s > 0) & (nt_prev > 0)
    cast_rows = wf.shape[1] // MOE_CAST_SLICES

    def w_start(ex, kk, slot):
        @pl.when(kk < n1)
        def _():
            col = pl.multiple_of(kk * MOE_TF, MOE_TF)
            pltpu.make_async_copy(wgu_hbm.at[layer, ex, :, pl.ds(col, MOE_TF)],
                                  wf.at[slot, :, pl.ds(0, MOE_TF)], sem_w.at[slot]).start()
            pltpu.make_async_copy(wgu_hbm.at[layer, ex, :, pl.ds(f + col, MOE_TF)],
                                  wf.at[slot, :, pl.ds(MOE_TF, MOE_TF)], sem_w.at[slot]).start()

        @pl.when(kk >= n1)
        def _():
            col = pl.multiple_of((kk - n1) * MOE_TN, MOE_TN)
            pltpu.make_async_copy(wd_hbm.at[layer, ex, :, pl.ds(col, MOE_TN)], wf.at[slot], sem_w.at[slot]).start()

    def w_wait(slot):
        pltpu.make_async_copy(wd_hbm.at[layer, 0, :, pl.ds(0, MOE_TN)], wf.at[slot], sem_w.at[slot]).wait()

    def cast_slice(slot, q):
        r0 = q * cast_rows if isinstance(q, int) else pl.multiple_of(q * cast_rows, cast_rows)
        wb[slot, pl.ds(r0, cast_rows), :] = wf[slot, pl.ds(r0, cast_rows), :].astype(BF16)

    def x_copy(tile, slot):
        return pltpu.make_async_copy(xs_hbm.at[pl.ds(tile * MOE_SUB, MOE_SUB), :], stage.at[slot], sem_x.at[slot])

    def x_land(slot, j):
        r0 = pl.multiple_of(j * MOE_SUB, MOE_SUB)
        xb[pl.ds(r0, MOE_SUB), :] = stage[slot].astype(BF16)

    def copy_out(n, j):
        col = n * MOE_TN if isinstance(n, int) else pl.multiple_of(n * MOE_TN, MOE_TN)
        return pltpu.make_async_copy(
            out.at[n, pl.ds(j * MOE_SUB, MOE_SUB), :],
            ys_hbm.at[pl.ds((t0 + j) * MOE_SUB, MOE_SUB), pl.ds(col, MOE_TN)],
            sem_out.at[n])

    def sub_tiles(fn, other):
        def trip(p, carry):
            for u in range(MOE_GROUP):
                fn(MOE_GROUP * p + u)
            cast_slice(other, p)
            return carry

        full = nt // MOE_GROUP
        lax.fori_loop(0, full, trip, 0)
        rem = nt - full * MOE_GROUP
        for r in range(1, MOE_GROUP):
            @pl.when(rem == r)
            def _():
                for u in range(r):
                    fn(full * MOE_GROUP + u)
                cast_slice(other, full)

        trips = full + (rem > 0).astype(jnp.int32)
        for q in range(MOE_CAST_SLICES):
            @pl.when(q >= trips)
            def _():
                cast_slice(other, q)

    def advance(k):
        slot = k % 2
        k2 = k + 2

        @pl.when(k2 < nch)
        def _():
            w_start(e, k2, slot)

        @pl.when((k2 >= nch) & has_next)
        def _():
            w_start(e_next, k2 - nch, slot)

        @pl.when((k + 1 < nch) | has_next)
        def _():
            w_wait(1 - slot)
        return slot

    @pl.when(nz > 0)
    def _():
        out[0, pl.ds(0, MOE_SUB), :] = jnp.zeros((MOE_SUB, MOE_TN), out.dtype)

        def copy_zero(n, j):
            return pltpu.make_async_copy(
                out.at[0, pl.ds(0, MOE_SUB), :],
                ys_hbm.at[pl.ds((t0 + j) * MOE_SUB, MOE_SUB), pl.ds(n * MOE_TN, MOE_TN)], sem_out.at[n])

        for j in range(MOE_ITEM_SUBS):
            @pl.when(j < nz)
            def _():
                for n in range(n2):
                    copy_zero(n, j).start()
        for j in range(MOE_ITEM_SUBS):
            @pl.when(j < nz)
            def _():
                for n in range(n2):
                    copy_zero(n, j).wait()

    @pl.when((s == 0) & (nt > 0))
    def _():
        w_start(e, 0, 0)
        w_start(e, 1, 1)
        x_copy(t0, 0).start()
        for j in range(MOE_ITEM_SUBS):
            @pl.when(j < nt)
            def _():
                if j + 1 < MOE_ITEM_SUBS:
                    @pl.when(j + 1 < nt)
                    def _():
                        x_copy(t0 + j + 1, (j + 1) % 2).start()
                x_copy(t0 + j, j % 2).wait()
                x_land(j % 2, j)
        w_wait(0)
        for q in range(MOE_CAST_SLICES):
            cast_slice(0, q)

    @pl.when(nt > 0)
    def _():
        def up_step(k, carry):
            slot = advance(k)
            bg = bgu_ref[k]
            bl = bgu_ref[n1 + k]

            def up(j):
                r0 = pl.multiple_of(j * MOE_SUB, MOE_SUB)
                x = xb[pl.ds(r0, MOE_SUB), :]
                glu = jnp.minimum(_dot(x, wb[slot, :, pl.ds(0, MOE_TF)]) + bg, SWIGLU_LIMIT)
                lin = jnp.clip(_dot(x, wb[slot, :, pl.ds(MOE_TF, MOE_TF)]) + bl, -SWIGLU_LIMIT, SWIGLU_LIMIT)
                act[k, pl.ds(r0, MOE_SUB), :] = (glu * jax.nn.sigmoid(SWIGLU_ALPHA * glu) * (lin + 1.0)).astype(BF16)

            sub_tiles(up, 1 - slot)
            return carry

        lax.fori_loop(0, n1, up_step, 0)

        def down_step(n, carry):
            slot = advance(n1 + n)
            bd = bd_ref[n]
            for jj in range(2):
                @pl.when(has_next & (2 * n + jj < nt_next))
                def _():
                    x_copy(t0_next + 2 * n + jj, jj).start()

            def down(j):
                r0 = pl.multiple_of(j * MOE_SUB, MOE_SUB)
                a = jnp.concatenate([act[k, pl.ds(r0, MOE_SUB), :] for k in range(n1)], axis=1)
                out[n, pl.ds(r0, MOE_SUB), :] = _dot(a, wb[slot]) + bd

            for j in range(MOE_ITEM_SUBS):
                @pl.when(has_prev & (j < nt_prev))
                def _():
                    copy_out(n, j).wait()

            sub_tiles(down, 1 - slot)
            for j in range(MOE_ITEM_SUBS):
                @pl.when(j < nt)
                def _():
                    copy_out(n, j).start()
            for jj in range(2):
                @pl.when(has_next & (2 * n + jj < nt_next))
                def _():
                    x_copy(t0_next + 2 * n + jj, jj).wait()
                    x_land(jj, 2 * n + jj)
            return carry

        lax.fori_loop(0, n2, down_step, 0)
        for n in range(n2):
            for j in range(MOE_ITEM_SUBS):
                @pl.when(jnp.logical_not(has_next) & (j < nt))
                def _():
                    copy_out(n, j).wait()


def _moe_experts(xs, items, w_gu, b_gu, w_down, b_down, layer):
    rp, d = xs.shape
    n_layers, n_exp, _, f2 = w_gu.shape
    f = f2 // 2
    assert f == d and MOE_TN == 2 * MOE_TF and MOE_ITEM_SUBS <= 2 * (d // MOE_TN)
    assert d % MOE_CAST_SLICES == 0 and -(-MOE_ITEM_SUBS // MOE_GROUP) <= MOE_CAST_SLICES
    n1 = f // MOE_TF
    n2 = d // MOE_TN
    item_e, item_t0, item_nt, item_nz = items
    n_items = item_e.shape[0]
    tm = MOE_ITEM_SUBS * MOE_SUB
    kern = functools.partial(_moe_kernel, layer=layer, n1=n1, n2=n2, f=f)
    return pl.pallas_call(
        kern,
        grid_spec=pltpu.PrefetchScalarGridSpec(
            num_scalar_prefetch=4,
            grid=(n_items,),
            in_specs=[
                pl.BlockSpec(memory_space=pl.ANY),
                pl.BlockSpec(memory_space=pl.ANY),
                pl.BlockSpec(memory_space=pl.ANY),
                pl.BlockSpec((None, None, 2 * n1, 1, MOE_TF), lambda s, ie, it0, nt, nz: (layer, ie[s], 0, 0, 0)),
                pl.BlockSpec((None, None, n2, 1, MOE_TN), lambda s, ie, it0, nt, nz: (layer, ie[s], 0, 0, 0)),
            ],
            out_specs=pl.BlockSpec(memory_space=pl.ANY),
            scratch_shapes=[
                pltpu.VMEM((2, MOE_SUB, d), F32),
                pltpu.VMEM((tm, d), BF16),
                pltpu.VMEM((n1, tm, MOE_TF), BF16),
                pltpu.VMEM((n2, tm, MOE_TN), F32),
                pltpu.VMEM((2, d, MOE_TN), F32),
                pltpu.VMEM((2, d, MOE_TN), BF16),
                pltpu.SemaphoreType.DMA((2,)),
                pltpu.SemaphoreType.DMA((2,)),
                pltpu.SemaphoreType.DMA((n2,)),
            ],
        ),
        out_shape=jax.ShapeDtypeStruct((rp, d), F32),
        compiler_params=_cparams(("arbitrary",)),
        name="moe_experts",
    )(item_e, item_t0, item_nt, item_nz, xs, w_gu, w_down,
      b_gu.reshape(n_layers, n_exp, 2 * n1, 1, MOE_TF), b_down.reshape(n_layers, n_exp, n2, 1, MOE_TN))


def _moe_plan(meta, counts, n_experts, rows_padded):
    n = meta.shape[0]
    top_i = meta[:, :TOP_K]
    rank = meta[:, TOP_K:2 * TOP_K]
    counts = counts[0, :n_experts].astype(jnp.int32)
    ntiles = (counts + MOE_SUB - 1) // MOE_SUB
    tile_end = jnp.cumsum(ntiles)
    tile_start = tile_end - ntiles
    experts = jnp.arange(n_experts, dtype=jnp.int32)
    start_sel = jnp.sum(jnp.where(top_i[:, :, None] == experts[None, None, :], tile_start[None, None, :], 0), axis=-1)
    pos = start_sel * MOE_SUB + rank
    n_items_max = n_experts + (n * TOP_K // MOE_SUB + MOE_ITEM_SUBS - 1) // MOE_ITEM_SUBS
    items_e = (ntiles + MOE_ITEM_SUBS - 1) // MOE_ITEM_SUBS
    item_end = jnp.cumsum(items_e)
    item_start = item_end - items_e
    sidx = jnp.arange(n_items_max, dtype=jnp.int32)
    valid = sidx < item_end[-1]
    ie = jnp.sum((sidx[:, None] >= item_end[None, :]).astype(jnp.int32), axis=1)
    ie = jnp.minimum(ie, n_experts - 1)
    pick = ie[:, None] == experts[None, :]
    local = sidx - jnp.sum(jnp.where(pick, item_start[None, :], 0), axis=1)
    t0 = jnp.sum(jnp.where(pick, tile_start[None, :], 0), axis=1) + local * MOE_ITEM_SUBS
    nt = jnp.clip(jnp.sum(jnp.where(pick, ntiles[None, :], 0), axis=1) - local * MOE_ITEM_SUBS, 0, MOE_ITEM_SUBS)
    last_e = jnp.max(jnp.where(valid, ie, 0))
    ie = jnp.where(valid, ie, last_e).astype(jnp.int32)
    z0 = tile_end[-1] + (sidx - item_end[-1]) * MOE_ITEM_SUBS
    nz = jnp.clip(rows_padded // MOE_SUB - z0, 0, MOE_ITEM_SUBS)
    t0 = jnp.where(valid, t0, z0).astype(jnp.int32)
    nt = jnp.where(valid, nt, 0).astype(jnp.int32)
    nz = jnp.where(valid, 0, nz).astype(jnp.int32)
    total_tiles = rows_padded // MOE_SUB
    tail = tile_end[-1] + jnp.arange(total_tiles - n * TOP_K // MOE_SUB, dtype=jnp.int32)
    zero_tiles = jnp.concatenate([jnp.where(ntiles > 0, tile_end - 1, -1),
                                  jnp.where(tail < total_tiles, tail, -1)]).astype(jnp.int32)
    return pos.astype(jnp.int32), zero_tiles, (ie, t0, nt, nz)


def _combine_kernel(tc_ref, pos_ref, posn_ref, x_ref, w_ref, gf_ref, ys_hbm, *rest, split_tiles):
    o_refs, (buf, sem) = rest[:-2], rest[-2:]
    i = pl.program_id(0)
    n_tiles = pl.num_programs(0)
    rows = x_ref.shape[0]

    def issue(pref, slot):
        def body(r, carry):
            for k in range(TOP_K):
                p = pref[0, r * TOP_K + k]
                pltpu.make_async_copy(ys_hbm.at[pl.ds(p, 1), :], buf.at[slot, k, pl.ds(r, 1), :],
                                      sem.at[slot]).start()
            return carry

        lax.fori_loop(0, rows, body, 0, unroll=4)

    slot = i % 2

    @pl.when(i == 0)
    def _():
        issue(pos_ref, 0)

    @pl.when(i + 1 < n_tiles)
    def _():
        issue(posn_ref, 1 - slot)

    for k in range(TOP_K):
        pltpu.make_async_copy(ys_hbm.at[pl.ds(0, rows), :], buf.at[slot, k], sem.at[slot]).wait()
    w = w_ref[...]
    acc = w[:, 0:1] * buf[slot, 0]
    for k in range(1, TOP_K):
        acc = acc + w[:, k:k + 1] * buf[slot, k]
    val = x_ref[...] + gf_ref[...] * acc
    if split_tiles is None:
        o_refs[0][...] = val
    else:
        @pl.when(i < split_tiles)
        def _():
            o_refs[0][...] = val

        @pl.when(i >= split_tiles)
        def _():
            o_refs[1][...] = val


def _combine(x, ys, pos_tiles, wts, tile_cond, gf, split_rows=None):
    n, d = x.shape
    tm = TOKEN_TILE
    n_tiles = n // tm
    pos_spec = lambda f: pl.BlockSpec((None, 1, tm * TOP_K), f, memory_space=pltpu.SMEM)
    if split_rows is None:
        split_tiles = None
        out_specs = pl.BlockSpec((tm, d), lambda i, tc: (i, 0))
        out_shape = jax.ShapeDtypeStruct((n, d), F32)
    else:
        split_tiles = split_rows // tm
        out_specs = [pl.BlockSpec((tm, d), lambda i, tc: (jnp.minimum(i, split_tiles - 1), 0)),
                     pl.BlockSpec((tm, d), lambda i, tc: (jnp.maximum(i - split_tiles, 0), 0))]
        out_shape = [jax.ShapeDtypeStruct((split_rows, d), F32), jax.ShapeDtypeStruct((n - split_rows, d), F32)]
    return pl.pallas_call(
        functools.partial(_combine_kernel, split_tiles=split_tiles),
        grid_spec=pltpu.PrefetchScalarGridSpec(
            num_scalar_prefetch=1,
            grid=(n_tiles,),
            in_specs=[
                pos_spec(lambda i, tc: (i, 0, 0)),
                pos_spec(lambda i, tc: (jnp.minimum(i + 1, n_tiles - 1), 0, 0)),
                pl.BlockSpec((tm, d), lambda i, tc: (i, 0)),
                pl.BlockSpec((tm, LANES), lambda i, tc: (i, 0)),
                pl.BlockSpec((None, 1, d), lambda i, tc: (tc[i], 0, 0)),
                pl.BlockSpec(memory_space=pl.ANY),
            ],
            out_specs=out_specs,
            scratch_shapes=[
                pltpu.VMEM((2, TOP_K, tm, d), F32),
                pltpu.SemaphoreType.DMA((2,)),
            ],
        ),
        out_shape=out_shape,
        compiler_params=_cparams(("arbitrary",)),
        name="moe_combine",
    )(tile_cond, pos_tiles, pos_tiles, x, wts, gf, ys)


def _moe(x, h2, meta, wts, counts, tile_cond, gf, w_gu, b_gu, w_down, b_down, layer, split_rows=None):
    n, d = x.shape
    n_exp = w_gu.shape[1]
    rows_padded = n * TOP_K + n_exp * MOE_SUB
    pos, zero_tiles, items = _moe_plan(meta, counts, n_exp, rows_padded)
    pos_tiles = pos.reshape(n // TOKEN_TILE, 1, TOKEN_TILE * TOP_K)
    xs = _dispatch(h2, pos_tiles, zero_tiles, rows_padded)
    ys = _moe_experts(xs, items, w_gu, b_gu, w_down, b_down, layer)
    return _combine(x, ys, pos_tiles, wts, tile_cond, gf, split_rows)


def _tile_cond(ns, npr, seq, tm):
    return jnp.asarray(np.concatenate([1 + np.arange(ns // tm) // (seq // tm),
                                       np.zeros(npr // tm, np.int64)]).astype(np.int32))


def kernel(x_prompt, x_sample, cache_na_k, cache_na_v, state_lru, c, c_ctx, norm_mix, norm_ffn, w_ada, b_ada, ev_w_in, ev_w_out, na_q_norm, na_k_norm, na_rpb, lru_conv_w, lru_conv_b, lru_wa, lru_ba, lru_wx, lru_bx, lru_lambda, od_w_in, od_w_out, sgu_norm, sgu_w, sgu_b, router_w, router_b, moe_w_gu, moe_b_gu, moe_w_down, moe_b_down):
    bp, sp_len, d = x_prompt.shape
    bs, ss_len, _ = x_sample.shape
    depth = w_ada.shape[0]
    heads, dh = cache_na_k.shape[3], cache_na_k.shape[4]
    na_w = heads * dh
    lru_w = state_lru.shape[-1]
    n_exp = router_w.shape[-1]
    ns, npr = bs * ss_len, bp * sp_len
    assert bs == REC_SEQS and bp % REC_SEQS == 0
    assert ss_len % INPROJ_TILE == 0 and npr % INPROJ_TILE == 0 and sp_len % TOKEN_TILE == 0

    x_parts = [x_sample.reshape(ns, d), x_prompt.reshape(npr, d)]
    tc_in = _tile_cond(ns, npr, ss_len, INPROJ_TILE)
    tc_proj = _tile_cond(ns, npr, ss_len, ROW_TILE)
    tc_tok = _tile_cond(ns, npr, ss_len, TOKEN_TILE)
    cond_rows = 16
    cond = jnp.concatenate([c_ctx[None, :], c, jnp.zeros((cond_rows - 1 - bs, d), F32)], axis=0)
    mod = _adaln(cond, w_ada, b_ada)
    mod = mod.reshape(depth, cond_rows, 6, d).transpose(0, 2, 1, 3).reshape(depth, 6, cond_rows, 1, d)

    new_k, new_v, new_h = [], [], []
    for layer in range(depth):
        sm, scm, gm, sf, scf, gf = [mod[layer, i] for i in range(6)]
        g_mix = norm_mix[layer][None, :]
        g_ffn = norm_ffn[layer][None, :]
        if layer % 2 == 0:
            e = layer // 2
            ev_in = ev_w_in.shape[-1]
            head_gain = jnp.concatenate([jnp.tile(na_q_norm[e], heads), jnp.tile(na_k_norm[e], heads),
                                         jnp.ones((ev_in - 2 * na_w,), F32)])[None, :]
            proj = _inproj(x_parts, tc_in, g_mix, scm, sm, ev_w_in[e].astype(BF16), head_gain,
                           mode="heads", n_norm_cols=2 * na_w)
            att_s = _na_attention(proj, cache_na_k[:, e].reshape(bs, -1, na_w),
                                  cache_na_v[:, e].reshape(bs, -1, na_w), _na_bias_table(na_rpb[e]),
                                  nseq=bs, seq=ss_len, heads=heads, dh=dh)
            att_p = _ctx_attention(proj, row0=ns, nseq=bp, seq=sp_len, heads=heads, dh=dh)
            wgate, bgate = _pack_gate_weights(lru_wa[e], lru_ba[e], lru_wx[e], lru_bx[e])
            rec_args = (lru_conv_w[e], lru_conv_b[e][None, :], wgate, bgate, lru_lambda[e])
            h0_s = state_lru[:, e].transpose(1, 0, 2)[None]
            rec_s, _ = _recurrent(proj, *rec_args, h0_s, row0=0, nseq=bs, seq=ss_len,
                                  xr_col=3 * na_w, yg_col=3 * na_w + lru_w, width=lru_w)
            h0_p = jnp.zeros((bp // REC_SEQS, 2, REC_SEQS, lru_w), F32)
            rec_p, hl = _recurrent(proj, *rec_args, h0_p, row0=ns, nseq=bp, seq=sp_len,
                                   xr_col=3 * na_w, yg_col=3 * na_w + lru_w, width=lru_w)
            new_k.append(proj[ns:, na_w:2 * na_w].reshape(bp, sp_len, heads, dh))
            new_v.append(proj[ns:, 2 * na_w:3 * na_w].reshape(bp, sp_len, heads, dh))
            new_h.append(hl.transpose(0, 2, 1, 3).reshape(bp, 2, lru_w))
            w_out = ev_w_out[e].astype(BF16)
            lhs_list, w_list = [[att_s, att_p], [rec_s, rec_p]], [w_out[:na_w], w_out[na_w:]]
        else:
            o = layer // 2
            z = _inproj(x_parts, tc_in, g_mix, scm, sm, od_w_in[o].astype(BF16),
                        jnp.ones((1, od_w_in.shape[-1]), F32), mode="gelu", n_norm_cols=0)
            chunk = sgu_w.shape[-1]
            sg = _sgu(z, sgu_norm[o][None, :], sgu_w[o].astype(BF16), sgu_b[o].T, chunk=chunk)
            lhs_list, w_list = [[sg]], [od_w_out[o].astype(BF16)]
        rw = jnp.pad(router_w[layer], ((0, 0), (0, LANES - n_exp)))
        rw_hi, rw_lo = _split_bf16(rw)
        rb = jnp.pad(router_b[layer], (0, LANES - n_exp))[None, :]
        x, h2, meta, wts, counts = _outproj(lhs_list, w_list, x_parts, tc_proj, gm, g_ffn, scf, sf,
                                             rw_hi, rw_lo, rb, n_experts=n_exp)
        x = _moe(x, h2, meta, wts, counts, tc_tok, gf, moe_w_gu, moe_b_gu, moe_w_down, moe_b_down, layer,
                 split_rows=ns if layer == depth - 1 else None)
        x_parts = list(x) if layer == depth - 1 else [x]

    y_sample = x_parts[0].reshape(bs, ss_len, d)
    y_prompt = x_parts[1].reshape(bp, sp_len, d)
    new_na_k = jnp.stack(new_k, axis=1)
    new_na_v = jnp.stack(new_v, axis=1)
    new_state_lru = jnp.stack(new_h, axis=1)
    return (y_prompt, y_sample, new_na_k, new_na_v, new_state_lru)
```

```python
import functools

import numpy as np
import jax
import jax.numpy as jnp
from jax import lax
from jax.experimental import pallas as pl
from jax.experimental.pallas import tpu as pltpu

F32 = jnp.float32
BF16 = jnp.bfloat16

EPS = 1e-6
NEG_INF = -1e30
GRID_W = 64
NA_KH = 8
NA_KW = 16
LRU_C = 8.0
SWIGLU_ALPHA = 1.702
SWIGLU_LIMIT = 7.0
TOP_K = 4

LANES = 128
INPROJ_TILE = 1024
ROW_TILE = 512
PROJ_CHUNK = 256
TOKEN_TILE = 256
MOE_SUB = 256
MOE_ITEM_SUBS = 7
MOE_TF = 256
MOE_TN = 512
MOE_GROUP = 3
MOE_CAST_SLICES = 4
MOE_WBUF = 3
VMEM_LIMIT = 56 * 1024 * 1024


def _cparams(sem, vmem=VMEM_LIMIT):
    return pltpu.CompilerParams(dimension_semantics=sem, vmem_limit_bytes=vmem)


def _split_bf16(x):
    hi = x.astype(BF16)
    lo = (x - hi.astype(F32)).astype(BF16)
    return hi, lo


def _gelu_tanh(x):
    return 0.5 * x * (1.0 + jnp.tanh(0.7978845608028654 * (x + 0.044715 * x * x * x)))


def _dot(a, b):
    return jnp.dot(a, b, preferred_element_type=F32)


def _dot_nt(a, b):
    return lax.dot_general(a, b, (((1,), (1,)), ((), ())), preferred_element_type=F32)


def _adaln_kernel(c_ref, w_ref, b_ref, o_ref):
    cnd = c_ref[...]
    s = cnd * jax.nn.sigmoid(cnd)
    hi, lo = _split_bf16(s)
    w = w_ref[...].astype(BF16)
    o_ref[...] = _dot(hi, w) + _dot(lo, w) + b_ref[...]


def _adaln(cond, w_ada, b_ada, tn=1024):
    depth, d, n6 = w_ada.shape
    rows = cond.shape[0]
    return pl.pallas_call(
        _adaln_kernel,
        grid=(depth, n6 // tn),
        in_specs=[
            pl.BlockSpec((rows, d), lambda l, j: (0, 0)),
            pl.BlockSpec((None, d, tn), lambda l, j: (l, 0, j)),
            pl.BlockSpec((None, 1, tn), lambda l, j: (l, 0, j)),
        ],
        out_specs=pl.BlockSpec((None, rows, tn), lambda l, j: (l, 0, j)),
        out_shape=jax.ShapeDtypeStruct((depth, rows, n6), F32),
        compiler_params=_cparams(("arbitrary", "arbitrary")),
        name="adaln",
    )(cond, w_ada, b_ada.reshape(depth, 1, n6))


def _modulated_norm(x, g, sc, sh):
    ms = jnp.mean(x * x, axis=-1, keepdims=True)
    return (x * lax.rsqrt(ms + EPS) * g) * (1.0 + sc) + sh


def _row_specs(parts, tm, width, **spec_kwargs):
    specs, bounds, off = [], [], 0
    for a in parts:
        nt = a.shape[0] // tm
        specs.append(pl.BlockSpec((tm, width), lambda i, *_, off=off, nt=nt: (jnp.clip(i - off, 0, nt - 1), 0),
                                  **spec_kwargs))
        off += nt
        bounds.append(off)
    return specs, tuple(bounds)


def _row_select(refs, bounds, i, idx):
    val = refs[-1][idx]
    for p in range(len(refs) - 2, -1, -1):
        val = jnp.where(i < bounds[p], refs[p][idx], val)
    return val


def _inproj_kernel(tc_ref, *refs, mode, n_norm_tiles, bounds):
    x_refs = refs[:len(bounds)]
    g_ref, sc_ref, sh_ref, w_ref, hg_ref, o_ref, h_scr = refs[len(bounds):]
    i = pl.program_id(0)
    j = pl.program_id(1)

    @pl.when(j == 0)
    def _():
        for rc in range(h_scr.shape[0] // PROJ_CHUNK):
            rows = slice(rc * PROJ_CHUNK, (rc + 1) * PROJ_CHUNK)
            x = _row_select(x_refs, bounds, i, (rows, slice(None)))
            h_scr[rows, :] = _modulated_norm(x, g_ref[...], sc_ref[...], sh_ref[...]).astype(BF16)

    def chunks(epilogue):
        for c in range(o_ref.shape[1] // PROJ_CHUNK):
            cs = slice(c * PROJ_CHUNK, (c + 1) * PROJ_CHUNK)
            o_ref[:, cs] = epilogue(_dot(h_scr[...], w_ref[:, cs]), cs)

    def head_norm(acc, cs):
        hg = hg_ref[:, cs]
        parts = []
        for c in range(PROJ_CHUNK // LANES):
            blk = acc[:, c * LANES:(c + 1) * LANES]
            ms = jnp.mean(blk * blk, axis=-1, keepdims=True)
            parts.append(blk * lax.rsqrt(ms + EPS) * hg[:, c * LANES:(c + 1) * LANES])
        return jnp.concatenate(parts, axis=1)

    if mode == "gelu":
        chunks(lambda acc, cs: _gelu_tanh(acc))
    else:
        @pl.when(j < n_norm_tiles)
        def _():
            chunks(head_norm)

        @pl.when(j >= n_norm_tiles)
        def _():
            chunks(lambda acc, cs: acc)


def _inproj(x_parts, tile_cond, norm_g, sc, sh, w_bf16, head_gain, *, mode, n_norm_cols, tn=1024):
    n = sum(a.shape[0] for a in x_parts)
    d = x_parts[0].shape[1]
    nout = w_bf16.shape[1]
    tm = INPROJ_TILE
    x_specs, bounds = _row_specs(x_parts, tm, d, **({"pipeline_mode": pl.Buffered(1)} if len(x_parts) > 1 else {}))
    kern = functools.partial(_inproj_kernel, mode=mode, n_norm_tiles=n_norm_cols // tn, bounds=bounds)
    modspec = pl.BlockSpec((None, 1, d), lambda i, j, tc: (tc[i], 0, 0))
    return pl.pallas_call(
        kern,
        grid_spec=pltpu.PrefetchScalarGridSpec(
            num_scalar_prefetch=1,
            grid=(n // tm, nout // tn),
            in_specs=x_specs + [
                pl.BlockSpec((1, d), lambda i, j, tc: (0, 0)),
                modspec, modspec,
                pl.BlockSpec((d, tn), lambda i, j, tc: (0, j)),
                pl.BlockSpec((1, tn), lambda i, j, tc: (0, j)),
            ],
            out_specs=pl.BlockSpec((tm, tn), lambda i, j, tc: (i, j)),
            scratch_shapes=[pltpu.VMEM((tm, d), BF16)],
        ),
        out_shape=jax.ShapeDtypeStruct((n, nout), F32),
        compiler_params=_cparams(("arbitrary", "arbitrary")),
        name="inproj_" + mode,
    )(tile_cond, *x_parts, norm_g, sc, sh, w_bf16, head_gain)


def _ctx_attn_kernel(q_ref, k_ref, v_ref, o_ref, *, heads, dh):
    scale = dh ** -0.5
    for h in range(heads):
        sl = slice(h * dh, (h + 1) * dh)
        q = q_ref[:, sl].astype(BF16)
        k = k_ref[:, sl].astype(BF16)
        v = v_ref[:, sl].astype(BF16)
        s = _dot_nt(q, k) * scale
        m = jnp.max(s, axis=-1, keepdims=True)
        p = jnp.exp(s - m)
        den = jnp.sum(p, axis=-1, keepdims=True)
        o = _dot(p.astype(BF16), v) / den
        o_ref[:, sl] = o.astype(o_ref.dtype)


def _ctx_attention(proj, *, row0, nseq, seq, heads, dh):
    width = heads * dh
    rb0 = row0 // seq
    kern = functools.partial(_ctx_attn_kernel, heads=heads, dh=dh)
    return pl.pallas_call(
        kern,
        grid=(nseq,),
        in_specs=[
            pl.BlockSpec((seq, width), lambda b: (rb0 + b, 0)),
            pl.BlockSpec((seq, width), lambda b: (rb0 + b, 1)),
            pl.BlockSpec((seq, width), lambda b: (rb0 + b, 2)),
        ],
        out_specs=pl.BlockSpec((seq, width), lambda b: (b, 0)),
        out_shape=jax.ShapeDtypeStruct((nseq * seq, width), BF16),
        compiler_params=_cparams(("arbitrary",)),
        name="ctx_attention",
    )(proj, proj, proj)


def _na_attn_kernel(q_ref, k_ref, v_ref, kc_ref, vc_ref, bias_ref, o_ref, s_scr, p_scr, *, rows, dh):
    scale = dh ** -0.5
    win = NA_KH * GRID_W
    qc = lax.broadcasted_iota(jnp.int32, (GRID_W, win), 0)
    kc = lax.broadcasted_iota(jnp.int32, (GRID_W, win), 1) % GRID_W
    c_start = jnp.clip(qc - NA_KW // 2, 0, GRID_W - NA_KW)
    col_ok = (kc >= c_start) & (kc < c_start + NA_KW)
    v_ctx = vc_ref[...].astype(BF16)
    s_scr[:, win:] = _dot_nt(q_ref[...].astype(BF16), kc_ref[...].astype(BF16)) * scale
    for r in range(rows):
        rs = min(max(r - NA_KH // 2, 0), rows - NA_KH)
        q = q_ref[r * GRID_W:(r + 1) * GRID_W, :].astype(BF16)
        kw = k_ref[rs * GRID_W:rs * GRID_W + win, :].astype(BF16)
        s_win = _dot_nt(q, kw) * scale + bias_ref[r - rs]
        s_scr[r * GRID_W:(r + 1) * GRID_W, :win] = jnp.where(col_ok, s_win, NEG_INF)
    blk = 2 * GRID_W
    inv = []
    for c in range(s_scr.shape[0] // blk):
        s = s_scr[c * blk:(c + 1) * blk, :]
        p = jnp.exp(s - jnp.max(s, axis=-1, keepdims=True))
        inv.append(1.0 / jnp.sum(p, axis=-1, keepdims=True))
        p_scr[c * blk:(c + 1) * blk, :] = p.astype(BF16)
    o_ctx = _dot(p_scr[:, win:], v_ctx)
    for r in range(rows):
        rs = min(max(r - NA_KH // 2, 0), rows - NA_KH)
        rr = slice(r * GRID_W, (r + 1) * GRID_W)
        vw = v_ref[rs * GRID_W:rs * GRID_W + win, :].astype(BF16)
        o = _dot(p_scr[rr, :win], vw) + o_ctx[rr, :]
        half = inv[r // 2][(r % 2) * GRID_W:(r % 2 + 1) * GRID_W, :]
        o_ref[rr, :] = (o * half).astype(o_ref.dtype)


def _na_bias_kernel(rpb_ref, o_ref, *, n_rows, n_cols):
    h = pl.program_id(0)
    q = lax.broadcasted_iota(jnp.int32, (GRID_W, GRID_W), 0)
    k = lax.broadcasted_iota(jnp.int32, (GRID_W, GRID_W), 1)
    col_idx = jnp.clip(k - q + NA_KW - 1, 0, n_cols - 1)
    for ri in range(n_rows):
        acc = jnp.zeros((GRID_W, GRID_W), F32)
        for j in range(n_cols):
            acc = jnp.where(col_idx == j, rpb_ref[(h * n_rows + ri) * n_cols + j], acc)
        o_ref[ri] = acc


def _na_bias_table(rpb):
    heads, n_rows, n_cols = rpb.shape
    toep = pl.pallas_call(
        functools.partial(_na_bias_kernel, n_rows=n_rows, n_cols=n_cols),
        grid=(heads,),
        in_specs=[pl.BlockSpec(memory_space=pltpu.SMEM)],
        out_specs=pl.BlockSpec((None, n_rows, GRID_W, GRID_W), lambda h: (h, 0, 0, 0)),
        out_shape=jax.ShapeDtypeStruct((heads, n_rows, GRID_W, GRID_W), F32),
        compiler_params=_cparams(("arbitrary",)),
        name="na_bias",
    )(rpb.reshape(-1))
    classes = []
    for d in range(NA_KH):
        t = toep[:, NA_KH - 1 - d:2 * NA_KH - 1 - d]
        classes.append(t.transpose(0, 2, 1, 3).reshape(heads, GRID_W, NA_KH * GRID_W))
    return jnp.stack(classes, axis=1)


def _na_attention(proj, cache_k, cache_v, bias_tab, *, nseq, seq, heads, dh):
    past = cache_k.shape[1]
    rows = seq // GRID_W
    kern = functools.partial(_na_attn_kernel, rows=rows, dh=dh)
    return pl.pallas_call(
        kern,
        grid=(heads, nseq),
        in_specs=[
            pl.BlockSpec((seq, dh), lambda h, b: (b, h)),
            pl.BlockSpec((seq, dh), lambda h, b: (b, heads + h)),
            pl.BlockSpec((seq, dh), lambda h, b: (b, 2 * heads + h)),
            pl.BlockSpec((None, past, dh), lambda h, b: (b, 0, h)),
            pl.BlockSpec((None, past, dh), lambda h, b: (b, 0, h)),
            pl.BlockSpec((None, NA_KH, GRID_W, NA_KH * GRID_W), lambda h, b: (h, 0, 0, 0)),
        ],
        out_specs=pl.BlockSpec((seq, dh), lambda h, b: (b, h)),
        out_shape=jax.ShapeDtypeStruct((nseq * seq, heads * dh), BF16),
        scratch_shapes=[
            pltpu.VMEM((seq, NA_KH * GRID_W + past), F32),
            pltpu.VMEM((seq, NA_KH * GRID_W + past), BF16),
        ],
        compiler_params=_cparams(("arbitrary", "arbitrary")),
        name="na_attention",
    )(proj, proj, proj, cache_k, cache_v, bias_tab)


REC_SEQS = 8
REC_CHUNK = 256


def _rec_kernel(xr_ref, yg_ref, cw_ref, cb_ref, wg_ref, bg_ref, lam_ref, h0_ref,
                out_ref, hl_ref, a_scr, b_scr, *, seq):
    pitch = seq + 8
    total = REC_SEQS * seq
    nchunks = total // REC_CHUNK
    row = lax.broadcasted_iota(jnp.int32, (REC_CHUNK, LANES), 0)
    cw = cw_ref[...]
    cb = cb_ref[...]
    wg_hi = wg_ref[0]
    wg_lo = wg_ref[1]
    bg = bg_ref[...]
    lam = lam_ref[...]
    sp = jnp.maximum(-lam, 0.0) + jnp.log(1.0 + jnp.exp(-jnp.abs(lam)))

    def coeff_chunk(ci, carry):
        r0 = pl.multiple_of(ci * REC_CHUNK, REC_CHUNK)
        t0 = r0 % seq
        x = xr_ref[pl.ds(r0, REC_CHUNK), :]
        prev = xr_ref[pl.ds(pl.multiple_of(jnp.maximum(r0 - 8, 0), 8), 8), :]
        nxt = xr_ref[pl.ds(pl.multiple_of(jnp.minimum(r0 + REC_CHUNK, total - 8), 8), 8), :]
        has_prev = (t0 > 0).astype(F32)
        has_next = (t0 + REC_CHUNK < seq).astype(F32)
        p6 = prev[6:7, :] * has_prev
        p7 = prev[7:8, :] * has_prev
        n0 = nxt[0:1, :] * has_next
        x_m1 = jnp.where(row == 0, p7, pltpu.roll(x, 1, axis=0))
        x_m2 = jnp.where(row == 0, p6, jnp.where(row == 1, p7, pltpu.roll(x, 2, axis=0)))
        x_p1 = jnp.where(row == REC_CHUNK - 1, n0, pltpu.roll(x, REC_CHUNK - 1, axis=0))
        xc = cb + x_m2 * cw[0:1, :] + x_m1 * cw[1:2, :] + x * cw[2:3, :] + x_p1 * cw[3:4, :]
        xh, xl = _split_bf16(xc)
        gates = _dot(xh, wg_hi) + _dot(xl, wg_hi) + _dot(xh, wg_lo) + bg
        srow = pl.multiple_of((r0 // seq) * pitch + t0, 8)
        for d in range(2):
            r = jax.nn.sigmoid(gates[:, (2 * d) * LANES:(2 * d + 1) * LANES])
            i = jax.nn.sigmoid(gates[:, (2 * d + 1) * LANES:(2 * d + 2) * LANES])
            log_a = (-LRU_C) * r * sp[d:d + 1, :]
            a = jnp.exp(log_a)
            b = jnp.sqrt(1.0 - jnp.exp(2.0 * log_a)) * (i * xc)
            a_scr[d, pl.ds(srow, REC_CHUNK), :] = a
            b_scr[d, pl.ds(srow, REC_CHUNK), :] = b
        return carry

    lax.fori_loop(0, nchunks, coeff_chunk, 0)

    def scan_step(t, carry):
        hf, hb = carry
        tb = seq - 1 - t
        hf = a_scr[0, pl.ds(t, REC_SEQS, stride=pitch), :] * hf + b_scr[0, pl.ds(t, REC_SEQS, stride=pitch), :]
        b_scr[0, pl.ds(t, REC_SEQS, stride=pitch), :] = hf
        hb = a_scr[1, pl.ds(tb, REC_SEQS, stride=pitch), :] * hb + b_scr[1, pl.ds(tb, REC_SEQS, stride=pitch), :]
        b_scr[1, pl.ds(tb, REC_SEQS, stride=pitch), :] = hb
        return hf, hb

    hf, hb = lax.fori_loop(0, seq, scan_step, (h0_ref[0], h0_ref[1]), unroll=8)
    hl_ref[0] = hf
    hl_ref[1] = hb

    def out_chunk(ci, carry):
        r0 = pl.multiple_of(ci * REC_CHUNK, REC_CHUNK)
        srow = pl.multiple_of((r0 // seq) * pitch + r0 % seq, 8)
        hsum = b_scr[0, pl.ds(srow, REC_CHUNK), :] + b_scr[1, pl.ds(srow, REC_CHUNK), :]
        out_ref[pl.ds(r0, REC_CHUNK), :] = (_gelu_tanh(yg_ref[pl.ds(r0, REC_CHUNK), :]) * hsum).astype(out_ref.dtype)
        return carry

    lax.fori_loop(0, nchunks, out_chunk, 0)


def _recurrent(proj, conv_w, conv_b, wgate, bgate, lam, h0, *, row0, nseq, seq, xr_col, yg_col, width):
    groups = nseq // REC_SEQS
    blk_rows = REC_SEQS * seq
    rb0 = row0 // blk_rows
    nblk = width // LANES
    xc0 = xr_col // LANES
    yc0 = yg_col // LANES
    pitch = seq + 8
    kern = functools.partial(_rec_kernel, seq=seq)
    return pl.pallas_call(
        kern,
        grid=(groups, nblk),
        in_specs=[
            pl.BlockSpec((blk_rows, LANES), lambda g, n: (rb0 + g, xc0 + n)),
            pl.BlockSpec((blk_rows, LANES), lambda g, n: (rb0 + g, yc0 + n)),
            pl.BlockSpec((4, LANES), lambda g, n: (0, n)),
            pl.BlockSpec((1, LANES), lambda g, n: (0, n)),
            pl.BlockSpec((None, 2, LANES, 4 * LANES), lambda g, n: (n, 0, 0, 0)),
            pl.BlockSpec((None, 1, 4 * LANES), lambda g, n: (n, 0, 0)),
            pl.BlockSpec((2, LANES), lambda g, n: (0, n)),
            pl.BlockSpec((None, 2, REC_SEQS, LANES), lambda g, n: (g, 0, 0, n)),
        ],
        out_specs=[
            pl.BlockSpec((blk_rows, LANES), lambda g, n: (g, n)),
            pl.BlockSpec((None, 2, REC_SEQS, LANES), lambda g, n: (g, 0, 0, n)),
        ],
        out_shape=[
            jax.ShapeDtypeStruct((nseq * seq, width), BF16),
            jax.ShapeDtypeStruct((groups, 2, REC_SEQS, width), F32),
        ],
        scratch_shapes=[
            pltpu.VMEM((2, REC_SEQS * pitch, LANES), F32),
            pltpu.VMEM((2, REC_SEQS * pitch, LANES), F32),
        ],
        compiler_params=_cparams(("arbitrary", "arbitrary")),
        name="recurrent",
    )(proj, proj, conv_w, conv_b, wgate, bgate, lam, h0)


def _pack_gate_weights(wa, ba, wx, bx):
    nblk = wa.shape[1]
    w = jnp.concatenate([wa[0], wx[0], wa[1], wx[1]], axis=-1)
    hi, lo = _split_bf16(w)
    b = jnp.stack([ba[0].reshape(nblk, LANES), bx[0].reshape(nblk, LANES),
                   ba[1].reshape(nblk, LANES), bx[1].reshape(nblk, LANES)], axis=1)
    return jnp.stack([hi, lo], axis=1), b.reshape(nblk, 1, 4 * LANES)


def _sgu_kernel(u_ref, v_ref, g_ref, ws_ref, bs_ref, o_ref, *, chunk, groups):
    g = g_ref[...]
    for c in range(u_ref.shape[0] // chunk):
        rs = slice(c * chunk, (c + 1) * chunk)
        v = v_ref[rs, :]
        ms = jnp.mean(v * v, axis=-1, keepdims=True)
        vn = (v * lax.rsqrt(ms + EPS) * g).astype(BF16)
        for grp in range(groups):
            cs = slice(grp * LANES, (grp + 1) * LANES)
            mixed = _dot(ws_ref[grp], vn[:, cs]) + bs_ref[:, grp:grp + 1]
            o_ref[rs, cs] = (u_ref[rs, cs] * mixed).astype(o_ref.dtype)


def _sgu(z, v_norm, ws_bf16, bs_t, *, chunk, tm=512):
    n, w2 = z.shape
    w = w2 // 2
    groups = ws_bf16.shape[0]
    kern = functools.partial(_sgu_kernel, chunk=chunk, groups=groups)
    return pl.pallas_call(
        kern,
        grid=(n // tm,),
        in_specs=[
            pl.BlockSpec((tm, w), lambda i: (i, 0)),
            pl.BlockSpec((tm, w), lambda i: (i, 1)),
            pl.BlockSpec((1, w), lambda i: (0, 0)),
            pl.BlockSpec((groups, chunk, chunk), lambda i: (0, 0, 0)),
            pl.BlockSpec((chunk, groups), lambda i: (0, 0)),
        ],
        out_specs=pl.BlockSpec((tm, w), lambda i: (i, 0)),
        out_shape=jax.ShapeDtypeStruct((n, w), BF16),
        compiler_params=_cparams(("arbitrary",)),
        name="sgu",
    )(z, z, v_norm, ws_bf16, bs_t)


def _outproj_kernel(tc_ref, *refs, lhs_bounds, x_bounds, n_experts):
    tile = pl.program_id(0)
    refs = list(refs)
    lhs = [[refs.pop(0) for _ in b] for b in lhs_bounds]
    ws = [refs.pop(0) for _ in lhs_bounds]
    x_refs = [refs.pop(0) for _ in x_bounds]
    (gm_ref, g2_ref, scf_ref, sf_ref, rwh_ref, rwl_ref, rb_ref, tri_ref,
     xo_ref, h2_ref, meta_ref, wts_ref, cnt_ref, carry) = refs
    @pl.when(pl.program_id(0) == 0)
    def _():
        carry[...] = jnp.zeros(carry.shape, carry.dtype)

    count = carry[...]
    chunk = tri_ref.shape[0]
    for rc in range(xo_ref.shape[0] // chunk):
        rows = slice(rc * chunk, (rc + 1) * chunk)
        idx = (rows, slice(None))
        mix = _dot(_row_select(lhs[0], lhs_bounds[0], tile, idx), ws[0][...])
        for parts, bounds, w_ref in zip(lhs[1:], lhs_bounds[1:], ws[1:]):
            mix = mix + _dot(_row_select(parts, bounds, tile, idx), w_ref[...])
        xn = _row_select(x_refs, x_bounds, tile, idx) + gm_ref[...] * mix
        xo_ref[rows, :] = xn
        h2 = _modulated_norm(xn, g2_ref[...], scf_ref[...], sf_ref[...])
        hi, lo = _split_bf16(h2)
        h2_ref[rows, :] = h2
        logits = _dot(hi, rwh_ref[...]) + _dot(lo, rwh_ref[...]) + _dot(hi, rwl_ref[...]) + rb_ref[...]
        lane = lax.broadcasted_iota(jnp.int32, logits.shape, 1)
        l = jnp.where(lane < n_experts, logits, -jnp.inf)
        vals, idxs = [], []
        for _ in range(TOP_K):
            m = jnp.max(l, axis=-1, keepdims=True)
            i = jnp.min(jnp.where(l == m, lane, LANES), axis=-1, keepdims=True)
            vals.append(m)
            idxs.append(i)
            l = jnp.where(lane == i, -jnp.inf, l)
        es = [jnp.exp(v - vals[0]) for v in vals]
        den = es[0] + es[1] + es[2] + es[3]
        chosen = jnp.zeros(logits.shape, F32)
        for k in range(TOP_K):
            chosen = chosen + (lane == idxs[k]).astype(F32)
        before = count + _dot(tri_ref[...], chosen.astype(BF16))
        meta = jnp.zeros(logits.shape, jnp.int32)
        wts_out = jnp.zeros(logits.shape, F32)
        for k in range(TOP_K):
            rank = jnp.sum(jnp.where(lane == idxs[k], before, 0.0), axis=-1, keepdims=True).astype(jnp.int32)
            meta = jnp.where(lane == k, idxs[k], meta)
            meta = jnp.where(lane == TOP_K + k, rank, meta)
            wts_out = jnp.where(lane == k, es[k] / den, wts_out)
        meta_ref[rows, :] = meta
        wts_ref[rows, :] = wts_out
        count = count + jnp.sum(chosen, axis=0, keepdims=True)
    carry[...] = count
    cnt_ref[...] = count


def _outproj(lhs_list, w_list, x_parts, tile_cond, gm, g2, scf, sf, rw_hi, rw_lo, rb, *, n_experts):
    n = sum(a.shape[0] for a in x_parts)
    d = x_parts[0].shape[1]
    tm = ROW_TILE
    lhs_specs, lhs_bounds = [], []
    for parts in lhs_list:
        specs, bounds = _row_specs(parts, tm, parts[0].shape[1])
        lhs_specs += specs
        lhs_bounds.append(bounds)
    x_specs, x_bounds = _row_specs(x_parts, tm, d)
    kern = functools.partial(_outproj_kernel, lhs_bounds=tuple(lhs_bounds), x_bounds=x_bounds, n_experts=n_experts)
    modspec = pl.BlockSpec((None, 1, d), lambda i, tc: (tc[i], 0, 0))
    tri = jnp.asarray(np.tril(np.ones((tm, tm), np.float32), -1), BF16)
    in_specs = (
        lhs_specs
        + [pl.BlockSpec(w.shape, lambda i, tc: (0, 0)) for w in w_list]
        + x_specs
        + [modspec,
           pl.BlockSpec((1, d), lambda i, tc: (0, 0)),
           modspec, modspec,
           pl.BlockSpec((d, LANES), lambda i, tc: (0, 0)),
           pl.BlockSpec((d, LANES), lambda i, tc: (0, 0)),
           pl.BlockSpec((1, LANES), lambda i, tc: (0, 0)),
           pl.BlockSpec((tm, tm), lambda i, tc: (0, 0))])
    return pl.pallas_call(
        kern,
        grid_spec=pltpu.PrefetchScalarGridSpec(
            num_scalar_prefetch=1,
            grid=(n // tm,),
            in_specs=in_specs,
            out_specs=[
                pl.BlockSpec((tm, d), lambda i, tc: (i, 0)),
                pl.BlockSpec((tm, d), lambda i, tc: (i, 0)),
                pl.BlockSpec((tm, LANES), lambda i, tc: (i, 0)),
                pl.BlockSpec((tm, LANES), lambda i, tc: (i, 0)),
                pl.BlockSpec((1, LANES), lambda i, tc: (0, 0)),
            ],
            scratch_shapes=[pltpu.VMEM((1, LANES), F32)],
        ),
        out_shape=[
            jax.ShapeDtypeStruct((n, d), F32),
            jax.ShapeDtypeStruct((n, d), F32),
            jax.ShapeDtypeStruct((n, LANES), jnp.int32),
            jax.ShapeDtypeStruct((n, LANES), F32),
            jax.ShapeDtypeStruct((1, LANES), F32),
        ],
        compiler_params=_cparams(("arbitrary",)),
        name="outproj_router",
    )(tile_cond, *[a for parts in lhs_list for a in parts], *w_list, *x_parts,
      gm, g2, scf, sf, rw_hi, rw_lo, rb, tri)


def _dispatch_kernel(zt_ref, pos_ref, h_ref, xs_hbm, zbuf, sem, zsem):
    rows = h_ref.shape[0]

    @pl.when(pl.program_id(0) == 0)
    def _():
        zbuf[...] = jnp.zeros(zbuf.shape, zbuf.dtype)

        def zero_tile(i):
            return pltpu.make_async_copy(zbuf, xs_hbm.at[pl.ds(zt_ref[i] * MOE_SUB, MOE_SUB), :], zsem)

        for i in range(zt_ref.shape[0]):
            @pl.when(zt_ref[i] >= 0)
            def _():
                zero_tile(i).start()
        for i in range(zt_ref.shape[0]):
            @pl.when(zt_ref[i] >= 0)
            def _():
                zero_tile(i).wait()

    def issue(r, carry):
        for k in range(TOP_K):
            p = pos_ref[0, r * TOP_K + k]
            pltpu.make_async_copy(h_ref.at[pl.ds(r, 1), :], xs_hbm.at[pl.ds(p, 1), :], sem).start()
        return carry

    lax.fori_loop(0, rows, issue, 0, unroll=4)
    for k in range(TOP_K):
        pltpu.make_async_copy(h_ref, xs_hbm.at[pl.ds(0, rows), :], sem).wait()


def _dispatch(h2, pos_tiles, zero_tiles, rows_padded):
    n, d = h2.shape
    tm = TOKEN_TILE
    return pl.pallas_call(
        _dispatch_kernel,
        grid_spec=pltpu.PrefetchScalarGridSpec(
            num_scalar_prefetch=1,
            grid=(n // tm,),
            in_specs=[
                pl.BlockSpec((None, 1, tm * TOP_K), lambda i, zt: (i, 0, 0), memory_space=pltpu.SMEM),
                pl.BlockSpec((tm, d), lambda i, zt: (i, 0)),
            ],
            out_specs=pl.BlockSpec(memory_space=pl.ANY),
            scratch_shapes=[
                pltpu.VMEM((MOE_SUB, d), h2.dtype),
                pltpu.SemaphoreType.DMA,
                pltpu.SemaphoreType.DMA,
            ],
        ),
        out_shape=jax.ShapeDtypeStruct((rows_padded, d), h2.dtype),
        compiler_params=_cparams(("arbitrary",)),
        name="moe_dispatch",
    )(zero_tiles, pos_tiles, h2)


def _moe_kernel(ie_ref, it0_ref, int_ref, inz_ref, xs_hbm, wgu_hbm, wd_hbm, bgu_ref, bd_ref,
                ys_hbm, stage, xb, act, out, wf, wb, sem_x, sem_w, sem_out, *, layer, n1, n2, f):
    s = pl.program_id(0)
    n_items = pl.num_programs(0)
    nch = n1 + n2
    nt = int_ref[s]
    nz = inz_ref[s]
    t0 = it0_ref[s]
    e = ie_ref[s]
    sn = jnp.minimum(s + 1, n_items - 1)
    has_next = (s + 1 < n_items) & (int_ref[sn] > 0)
    e_next = ie_ref[sn]
    t0_next = it0_ref[sn]
    nt_next = int_ref[sn]
    nt_prev = int_ref[jnp.maximum(s - 1, 0)]
    has_prev = (s > 0) & (nt_prev > 0)
    cast_rows = wf.shape[1] // MOE_CAST_SLICES

    def w_start(ex, kk, slot):
        @pl.when(kk < n1)
        def _():
            col = pl.multiple_of(kk * MOE_TF, MOE_TF)
            pltpu.make_async_copy(wgu_hbm.at[layer, ex, :, pl.ds(col, MOE_TF)],
                                  wf.at[slot, :, pl.ds(0, MOE_TF)], sem_w.at[slot]).start()
            pltpu.make_async_copy(wgu_hbm.at[layer, ex, :, pl.ds(f + col, MOE_TF)],
                                  wf.at[slot, :, pl.ds(MOE_TF, MOE_TF)], sem_w.at[slot]).start()

        @pl.when(kk >= n1)
        def _():
            col = pl.multiple_of((kk - n1) * MOE_TN, MOE_TN)
            pltpu.make_async_copy(wd_hbm.at[layer, ex, :, pl.ds(col, MOE_TN)], wf.at[slot], sem_w.at[slot]).start()

    def w_wait(slot):
        pltpu.make_async_copy(wd_hbm.at[layer, 0, :, pl.ds(0, MOE_TN)], wf.at[slot], sem_w.at[slot]).wait()

    def cast_slice(slots, q):
        dst, src = slots
        r0 = q * cast_rows if isinstance(q, int) else pl.multiple_of(q * cast_rows, cast_rows)
        wb[dst, pl.ds(r0, cast_rows), :] = wf[src, pl.ds(r0, cast_rows), :].astype(BF16)

    def x_copy(tile, slot):
        return pltpu.make_async_copy(xs_hbm.at[pl.ds(tile * MOE_SUB, MOE_SUB), :], stage.at[slot], sem_x.at[slot])

    def x_land(slot, j):
        r0 = pl.multiple_of(j * MOE_SUB, MOE_SUB)
        xb[pl.ds(r0, MOE_SUB), :] = stage[slot].astype(BF16)

    def copy_out(n, j):
        col = n * MOE_TN if isinstance(n, int) else pl.multiple_of(n * MOE_TN, MOE_TN)
        return pltpu.make_async_copy(
            out.at[n, pl.ds(j * MOE_SUB, MOE_SUB), :],
            ys_hbm.at[pl.ds((t0 + j) * MOE_SUB, MOE_SUB), pl.ds(col, MOE_TN)],
            sem_out.at[n])

    def sub_tiles(fn, other):
        def trip(p, carry):
            for u in range(MOE_GROUP):
                fn(MOE_GROUP * p + u)
            cast_slice(other, p)
            return carry

        full = nt // MOE_GROUP
        lax.fori_loop(0, full, trip, 0)
        rem = nt - full * MOE_GROUP
        for r in range(1, MOE_GROUP):
            @pl.when(rem == r)
            def _():
                for u in range(r):
                    fn(full * MOE_GROUP + u)
                cast_slice(other, full)

        trips = full + (rem > 0).astype(jnp.int32)
        for q in range(MOE_CAST_SLICES):
            @pl.when(q >= trips)
            def _():
                cast_slice(other, q)

    def advance(k):
        slot = k % 2
        ahead = k + MOE_WBUF

        @pl.when(ahead < nch)
        def _():
            w_start(e, ahead, k % MOE_WBUF)

        @pl.when((ahead >= nch) & has_next)
        def _():
            w_start(e_next, ahead - nch, k % MOE_WBUF)

        landed = (k + 1) % MOE_WBUF

        @pl.when((k + 1 < nch) | has_next)
        def _():
            w_wait(landed)
        return slot, (1 - slot, landed)

    @pl.when(nz > 0)
    def _():
        out[0, pl.ds(0, MOE_SUB), :] = jnp.zeros((MOE_SUB, MOE_TN), out.dtype)

        def copy_zero(n, j):
            return pltpu.make_async_copy(
                out.at[0, pl.ds(0, MOE_SUB), :],
                ys_hbm.at[pl.ds((t0 + j) * MOE_SUB, MOE_SUB), pl.ds(n * MOE_TN, MOE_TN)], sem_out.at[n])

        for j in range(MOE_ITEM_SUBS):
            @pl.when(j < nz)
            def _():
                for n in range(n2):
                    copy_zero(n, j).start()
        for j in range(MOE_ITEM_SUBS):
            @pl.when(j < nz)
            def _():
                for n in range(n2):
                    copy_zero(n, j).wait()

    @pl.when((s == 0) & (nt > 0))
    def _():
        for q in range(MOE_WBUF):
            w_start(e, q, q)
        x_copy(t0, 0).start()
        for j in range(MOE_ITEM_SUBS):
            @pl.when(j < nt)
            def _():
                if j + 1 < MOE_ITEM_SUBS:
                    @pl.when(j + 1 < nt)
                    def _():
                        x_copy(t0 + j + 1, (j + 1) % 2).start()
                x_copy(t0 + j, j % 2).wait()
                x_land(j % 2, j)
        w_wait(0)
        for q in range(MOE_CAST_SLICES):
            cast_slice((0, 0), q)

    @pl.when(nt > 0)
    def _():
        def up_step(k, carry):
            slot, other = advance(k)
            bg = bgu_ref[k]
            bl = bgu_ref[n1 + k]

            def up(j):
                r0 = pl.multiple_of(j * MOE_SUB, MOE_SUB)
                x = xb[pl.ds(r0, MOE_SUB), :]
                glu = jnp.minimum(_dot(x, wb[slot, :, pl.ds(0, MOE_TF)]) + bg, SWIGLU_LIMIT)
                lin = jnp.clip(_dot(x, wb[slot, :, pl.ds(MOE_TF, MOE_TF)]) + bl, -SWIGLU_LIMIT, SWIGLU_LIMIT)
                act[k, pl.ds(r0, MOE_SUB), :] = (glu * jax.nn.sigmoid(SWIGLU_ALPHA * glu) * (lin + 1.0)).astype(BF16)

            sub_tiles(up, other)
            return carry

        lax.fori_loop(0, n1, up_step, 0)

        def down_step(n, carry):
            slot, other = advance(n1 + n)
            bd = bd_ref[n]
            for jj in range(2):
                @pl.when(has_next & (2 * n + jj < nt_next))
                def _():
                    x_copy(t0_next + 2 * n + jj, jj).start()

            def down(j):
                r0 = pl.multiple_of(j * MOE_SUB, MOE_SUB)
                a = jnp.concatenate([act[k, pl.ds(r0, MOE_SUB), :] for k in range(n1)], axis=1)
                out[n, pl.ds(r0, MOE_SUB), :] = _dot(a, wb[slot]) + bd

            for j in range(MOE_ITEM_SUBS):
                @pl.when(has_prev & (j < nt_prev))
                def _():
                    copy_out(n, j).wait()

            sub_tiles(down, other)
            for j in range(MOE_ITEM_SUBS):
                @pl.when(j < nt)
                def _():
                    copy_out(n, j).start()
            for jj in range(2):
                @pl.when(has_next & (2 * n + jj < nt_next))
                def _():
                    x_copy(t0_next + 2 * n + jj, jj).wait()
                    x_land(jj, 2 * n + jj)
            return carry

        lax.fori_loop(0, n2, down_step, 0)
        for n in range(n2):
            for j in range(MOE_ITEM_SUBS):
                @pl.when(jnp.logical_not(has_next) & (j < nt))
                def _():
                    copy_out(n, j).wait()


def _moe_experts(xs, items, w_gu, b_gu, w_down, b_down, layer):
    rp, d = xs.shape
    n_layers, n_exp, _, f2 = w_gu.shape
    f = f2 // 2
    assert f == d and MOE_TN == 2 * MOE_TF and MOE_ITEM_SUBS <= 2 * (d // MOE_TN)
    assert (f // MOE_TF + d // MOE_TN) % (2 * MOE_WBUF) == 0
    assert d % MOE_CAST_SLICES == 0 and -(-MOE_ITEM_SUBS // MOE_GROUP) <= MOE_CAST_SLICES
    n1 = f // MOE_TF
    n2 = d // MOE_TN
    item_e, item_t0, item_nt, item_nz = items
    n_items = item_e.shape[0]
    tm = MOE_ITEM_SUBS * MOE_SUB
    kern = functools.partial(_moe_kernel, layer=layer, n1=n1, n2=n2, f=f)
    return pl.pallas_call(
        kern,
        grid_spec=pltpu.PrefetchScalarGridSpec(
            num_scalar_prefetch=4,
            grid=(n_items,),
            in_specs=[
                pl.BlockSpec(memory_space=pl.ANY),
                pl.BlockSpec(memory_space=pl.ANY),
                pl.BlockSpec(memory_space=pl.ANY),
                pl.BlockSpec((None, None, 2 * n1, 1, MOE_TF), lambda s, ie, it0, nt, nz: (layer, ie[s], 0, 0, 0)),
                pl.BlockSpec((None, None, n2, 1, MOE_TN), lambda s, ie, it0, nt, nz: (layer, ie[s], 0, 0, 0)),
            ],
            out_specs=pl.BlockSpec(memory_space=pl.ANY),
            scratch_shapes=[
                pltpu.VMEM((2, MOE_SUB, d), F32),
                pltpu.VMEM((tm, d), BF16),
                pltpu.VMEM((n1, tm, MOE_TF), BF16),
                pltpu.VMEM((n2, tm, MOE_TN), F32),
                pltpu.VMEM((MOE_WBUF, d, MOE_TN), F32),
                pltpu.VMEM((2, d, MOE_TN), BF16),
                pltpu.SemaphoreType.DMA((2,)),
                pltpu.SemaphoreType.DMA((MOE_WBUF,)),
                pltpu.SemaphoreType.DMA((n2,)),
            ],
        ),
        out_shape=jax.ShapeDtypeStruct((rp, d), F32),
        compiler_params=_cparams(("arbitrary",)),
        name="moe_experts",
    )(item_e, item_t0, item_nt, item_nz, xs, w_gu, w_down,
      b_gu.reshape(n_layers, n_exp, 2 * n1, 1, MOE_TF), b_down.reshape(n_layers, n_exp, n2, 1, MOE_TN))


def _moe_plan(meta, counts, n_experts, rows_padded):
    n = meta.shape[0]
    top_i = meta[:, :TOP_K]
    rank = meta[:, TOP_K:2 * TOP_K]
    counts = counts[0, :n_experts].astype(jnp.int32)
    ntiles = (counts + MOE_SUB - 1) // MOE_SUB
    tile_end = jnp.cumsum(ntiles)
    tile_start = tile_end - ntiles
    experts = jnp.arange(n_experts, dtype=jnp.int32)
    start_sel = jnp.sum(jnp.where(top_i[:, :, None] == experts[None, None, :], tile_start[None, None, :], 0), axis=-1)
    pos = start_sel * MOE_SUB + rank
    n_items_max = n_experts + (n * TOP_K // MOE_SUB + MOE_ITEM_SUBS - 1) // MOE_ITEM_SUBS
    items_e = (ntiles + MOE_ITEM_SUBS - 1) // MOE_ITEM_SUBS
    item_end = jnp.cumsum(items_e)
    item_start = item_end - items_e
    sidx = jnp.arange(n_items_max, dtype=jnp.int32)
    valid = sidx < item_end[-1]
    ie = jnp.sum((sidx[:, None] >= item_end[None, :]).astype(jnp.int32), axis=1)
    ie = jnp.minimum(ie, n_experts - 1)
    pick = ie[:, None] == experts[None, :]
    local = sidx - jnp.sum(jnp.where(pick, item_start[None, :], 0), axis=1)
    t0 = jnp.sum(jnp.where(pick, tile_start[None, :], 0), axis=1) + local * MOE_ITEM_SUBS
    nt = jnp.clip(jnp.sum(jnp.where(pick, ntiles[None, :], 0), axis=1) - local * MOE_ITEM_SUBS, 0, MOE_ITEM_SUBS)
    last_e = jnp.max(jnp.where(valid, ie, 0))
    ie = jnp.where(valid, ie, last_e).astype(jnp.int32)
    z0 = tile_end[-1] + (sidx - item_end[-1]) * MOE_ITEM_SUBS
    nz = jnp.clip(rows_padded // MOE_SUB - z0, 0, MOE_ITEM_SUBS)
    t0 = jnp.where(valid, t0, z0).astype(jnp.int32)
    nt = jnp.where(valid, nt, 0).astype(jnp.int32)
    nz = jnp.where(valid, 0, nz).astype(jnp.int32)
    total_tiles = rows_padded // MOE_SUB
    tail = tile_end[-1] + jnp.arange(total_tiles - n * TOP_K // MOE_SUB, dtype=jnp.int32)
    zero_tiles = jnp.concatenate([jnp.where(ntiles > 0, tile_end - 1, -1),
                                  jnp.where(tail < total_tiles, tail, -1)]).astype(jnp.int32)
    return pos.astype(jnp.int32), zero_tiles, (ie, t0, nt, nz)


def _combine_kernel(tc_ref, pos_ref, posn_ref, x_ref, w_ref, gf_ref, ys_hbm, *rest, split_tiles):
    o_refs, (buf, sem) = rest[:-2], rest[-2:]
    i = pl.program_id(0)
    n_tiles = pl.num_programs(0)
    rows = x_ref.shape[0]

    def issue(pref, slot):
        def body(r, carry):
            for k in range(TOP_K):
                p = pref[0, r * TOP_K + k]
                pltpu.make_async_copy(ys_hbm.at[pl.ds(p, 1), :], buf.at[slot, k, pl.ds(r, 1), :],
                                      sem.at[slot]).start()
            return carry

        lax.fori_loop(0, rows, body, 0, unroll=4)

    slot = i % 2

    @pl.when(i == 0)
    def _():
        issue(pos_ref, 0)

    @pl.when(i + 1 < n_tiles)
    def _():
        issue(posn_ref, 1 - slot)

    for k in range(TOP_K):
        pltpu.make_async_copy(ys_hbm.at[pl.ds(0, rows), :], buf.at[slot, k], sem.at[slot]).wait()
    w = w_ref[...]
    acc = w[:, 0:1] * buf[slot, 0]
    for k in range(1, TOP_K):
        acc = acc + w[:, k:k + 1] * buf[slot, k]
    val = x_ref[...] + gf_ref[...] * acc
    if split_tiles is None:
        o_refs[0][...] = val
    else:
        @pl.when(i < split_tiles)
        def _():
            o_refs[0][...] = val

        @pl.when(i >= split_tiles)
        def _():
            o_refs[1][...] = val


def _combine(x, ys, pos_tiles, wts, tile_cond, gf, split_rows=None):
    n, d = x.shape
    tm = TOKEN_TILE
    n_tiles = n // tm
    pos_spec = lambda f: pl.BlockSpec((None, 1, tm * TOP_K), f, memory_space=pltpu.SMEM)
    if split_rows is None:
        split_tiles = None
        out_specs = pl.BlockSpec((tm, d), lambda i, tc: (i, 0))
        out_shape = jax.ShapeDtypeStruct((n, d), F32)
    else:
        split_tiles = split_rows // tm
        out_specs = [pl.BlockSpec((tm, d), lambda i, tc: (jnp.minimum(i, split_tiles - 1), 0)),
                     pl.BlockSpec((tm, d), lambda i, tc: (jnp.maximum(i - split_tiles, 0), 0))]
        out_shape = [jax.ShapeDtypeStruct((split_rows, d), F32), jax.ShapeDtypeStruct((n - split_rows, d), F32)]
    return pl.pallas_call(
        functools.partial(_combine_kernel, split_tiles=split_tiles),
        grid_spec=pltpu.PrefetchScalarGridSpec(
            num_scalar_prefetch=1,
            grid=(n_tiles,),
            in_specs=[
                pos_spec(lambda i, tc: (i, 0, 0)),
                pos_spec(lambda i, tc: (jnp.minimum(i + 1, n_tiles - 1), 0, 0)),
                pl.BlockSpec((tm, d), lambda i, tc: (i, 0)),
                pl.BlockSpec((tm, LANES), lambda i, tc: (i, 0)),
                pl.BlockSpec((None, 1, d), lambda i, tc: (tc[i], 0, 0)),
                pl.BlockSpec(memory_space=pl.ANY),
            ],
            out_specs=out_specs,
            scratch_shapes=[
                pltpu.VMEM((2, TOP_K, tm, d), F32),
                pltpu.SemaphoreType.DMA((2,)),
            ],
        ),
        out_shape=out_shape,
        compiler_params=_cparams(("arbitrary",)),
        name="moe_combine",
    )(tile_cond, pos_tiles, pos_tiles, x, wts, gf, ys)


def _moe(x, h2, meta, wts, counts, tile_cond, gf, w_gu, b_gu, w_down, b_down, layer, split_rows=None):
    n, d = x.shape
    n_exp = w_gu.shape[1]
    rows_padded = n * TOP_K + n_exp * MOE_SUB
    pos, zero_tiles, items = _moe_plan(meta, counts, n_exp, rows_padded)
    pos_tiles = pos.reshape(n // TOKEN_TILE, 1, TOKEN_TILE * TOP_K)
    xs = _dispatch(h2, pos_tiles, zero_tiles, rows_padded)
    ys = _moe_experts(xs, items, w_gu, b_gu, w_down, b_down, layer)
    return _combine(x, ys, pos_tiles, wts, tile_cond, gf, split_rows)


def _tile_cond(ns, npr, seq, tm):
    return jnp.asarray(np.concatenate([1 + np.arange(ns // tm) // (seq // tm),
                                       np.zeros(npr // tm, np.int64)]).astype(np.int32))


def kernel(x_prompt, x_sample, cache_na_k, cache_na_v, state_lru, c, c_ctx, norm_mix, norm_ffn, w_ada, b_ada, ev_w_in, ev_w_out, na_q_norm, na_k_norm, na_rpb, lru_conv_w, lru_conv_b, lru_wa, lru_ba, lru_wx, lru_bx, lru_lambda, od_w_in, od_w_out, sgu_norm, sgu_w, sgu_b, router_w, router_b, moe_w_gu, moe_b_gu, moe_w_down, moe_b_down):
    bp, sp_len, d = x_prompt.shape
    bs, ss_len, _ = x_sample.shape
    depth = w_ada.shape[0]
    heads, dh = cache_na_k.shape[3], cache_na_k.shape[4]
    na_w = heads * dh
    lru_w = state_lru.shape[-1]
    n_exp = router_w.shape[-1]
    ns, npr = bs * ss_len, bp * sp_len
    assert bs == REC_SEQS and bp % REC_SEQS == 0
    assert ss_len % INPROJ_TILE == 0 and npr % INPROJ_TILE == 0 and sp_len % TOKEN_TILE == 0

    x_parts = [x_sample.reshape(ns, d), x_prompt.reshape(npr, d)]
    tc_in = _tile_cond(ns, npr, ss_len, INPROJ_TILE)
    tc_proj = _tile_cond(ns, npr, ss_len, ROW_TILE)
    tc_tok = _tile_cond(ns, npr, ss_len, TOKEN_TILE)
    cond_rows = 16
    cond = jnp.concatenate([c_ctx[None, :], c, jnp.zeros((cond_rows - 1 - bs, d), F32)], axis=0)
    mod = _adaln(cond, w_ada, b_ada)
    mod = mod.reshape(depth, cond_rows, 6, d).transpose(0, 2, 1, 3).reshape(depth, 6, cond_rows, 1, d)

    new_k, new_v, new_h = [], [], []
    for layer in range(depth):
        sm, scm, gm, sf, scf, gf = [mod[layer, i] for i in range(6)]
        g_mix = norm_mix[layer][None, :]
        g_ffn = norm_ffn[layer][None, :]
        if layer % 2 == 0:
            e = layer // 2
            ev_in = ev_w_in.shape[-1]
            head_gain = jnp.concatenate([jnp.tile(na_q_norm[e], heads), jnp.tile(na_k_norm[e], heads),
                                         jnp.ones((ev_in - 2 * na_w,), F32)])[None, :]
            proj = _inproj(x_parts, tc_in, g_mix, scm, sm, ev_w_in[e].astype(BF16), head_gain,
                           mode="heads", n_norm_cols=2 * na_w)
            att_s = _na_attention(proj, cache_na_k[:, e].reshape(bs, -1, na_w),
                                  cache_na_v[:, e].reshape(bs, -1, na_w), _na_bias_table(na_rpb[e]),
                                  nseq=bs, seq=ss_len, heads=heads, dh=dh)
            att_p = _ctx_attention(proj, row0=ns, nseq=bp, seq=sp_len, heads=heads, dh=dh)
            wgate, bgate = _pack_gate_weights(lru_wa[e], lru_ba[e], lru_wx[e], lru_bx[e])
            rec_args = (lru_conv_w[e], lru_conv_b[e][None, :], wgate, bgate, lru_lambda[e])
            h0_s = state_lru[:, e].transpose(1, 0, 2)[None]
            rec_s, _ = _recurrent(proj, *rec_args, h0_s, row0=0, nseq=bs, seq=ss_len,
                                  xr_col=3 * na_w, yg_col=3 * na_w + lru_w, width=lru_w)
            h0_p = jnp.zeros((bp // REC_SEQS, 2, REC_SEQS, lru_w), F32)
            rec_p, hl = _recurrent(proj, *rec_args, h0_p, row0=ns, nseq=bp, seq=sp_len,
                                   xr_col=3 * na_w, yg_col=3 * na_w + lru_w, width=lru_w)
            new_k.append(proj[ns:, na_w:2 * na_w].reshape(bp, sp_len, heads, dh))
            new_v.append(proj[ns:, 2 * na_w:3 * na_w].reshape(bp, sp_len, heads, dh))
            new_h.append(hl.transpose(0, 2, 1, 3).reshape(bp, 2, lru_w))
            w_out = ev_w_out[e].astype(BF16)
            lhs_list, w_list = [[att_s, att_p], [rec_s, rec_p]], [w_out[:na_w], w_out[na_w:]]
        else:
            o = layer // 2
            z = _inproj(x_parts, tc_in, g_mix, scm, sm, od_w_in[o].astype(BF16),
                        jnp.ones((1, od_w_in.shape[-1]), F32), mode="gelu", n_norm_cols=0)
            chunk = sgu_w.shape[-1]
            sg = _sgu(z, sgu_norm[o][None, :], sgu_w[o].astype(BF16), sgu_b[o].T, chunk=chunk)
            lhs_list, w_list = [[sg]], [od_w_out[o].astype(BF16)]
        rw = jnp.pad(router_w[layer], ((0, 0), (0, LANES - n_exp)))
        rw_hi, rw_lo = _split_bf16(rw)
        rb = jnp.pad(router_b[layer], (0, LANES - n_exp))[None, :]
        x, h2, meta, wts, counts = _outproj(lhs_list, w_list, x_parts, tc_proj, gm, g_ffn, scf, sf,
                                             rw_hi, rw_lo, rb, n_experts=n_exp)
        x = _moe(x, h2, meta, wts, counts, tc_tok, gf, moe_w_gu, moe_b_gu, moe_w_down, moe_b_down, layer,
                 split_rows=ns if layer == depth - 1 else None)
        x_parts = list(x) if layer == depth - 1 else [x]

    y_sample = x_parts[0].reshape(bs, ss_len, d)
    y_prompt = x_parts[1].reshape(bp, sp_len, d)
    new_na_k = jnp.stack(new_k, axis=1)
    new_na_v = jnp.stack(new_v, axis=1)
    new_state_lru = jnp.stack(new_h, axis=1)
    return (y_prompt, y_sample, new_na_k, new_na_v, new_state_lru)
```
